```python
import jax, jax.numpy as jnp
from jax import lax
import numpy as np

D_MODEL = 2048
BATCH = 16
SEQ = 2048
DEPTH = 4

MIX_HALF = D_MODEL // 2
RET_HEADS = 4
RET_HEAD_DIM = MIX_HALF // RET_HEADS
RET_CHUNK = 128
RET_THETA = 10000.0
SGU_GROUPS = 4
SGU_GROUP_DIM = MIX_HALF // SGU_GROUPS
SGU_CHUNK = 128
POOL_WINDOWS = (2, 4, 8, 16)
POOL_GROUP_DIM = MIX_HALF // len(POOL_WINDOWS)
MOBA_HEADS = 8
MOBA_HEAD_DIM = MIX_HALF // MOBA_HEADS
MOBA_BLOCK = 256
MOBA_TOPK = 3
MOBA_QUERY_CHUNK = 4
ROPE_THETA = 500000.0
ROT_DIM = MOBA_HEAD_DIM // 4
D_FF = 5632
PLE_DIM = 256
N_NORMS = 8
N_EVEN = (DEPTH + 1) // 2
N_ODD = DEPTH // 2
EPS = 1e-6
NEG = -1e30

kernel_name = "hybrid_retention_sgu_pool_moba_macaron"


def rms_norm(x, gain=None):
    xf = x.astype(jnp.float32)
    y = xf * lax.rsqrt(jnp.mean(xf * xf, axis=-1, keepdims=True) + EPS)
    if gain is not None:
        y = y * gain.astype(jnp.float32)
    return y.astype(x.dtype)


def rotary(x, rot_dim, theta):
    s = x.shape[1]
    half = rot_dim // 2
    inv = 1.0 / jnp.power(jnp.float32(theta), jnp.arange(half, dtype=jnp.float32) / half)
    ang = jnp.arange(s, dtype=jnp.float32)[:, None] * inv[None, :]
    cos = jnp.cos(ang)[None, :, None, :]
    sin = jnp.sin(ang)[None, :, None, :]
    xf = x.astype(jnp.float32)
    x1 = xf[..., :half]
    x2 = xf[..., half:rot_dim]
    out = jnp.concatenate([x1 * cos - x2 * sin, x2 * cos + x1 * sin, xf[..., rot_dim:]], axis=-1)
    return out.astype(x.dtype)


def swiglu(h, wg, wu, wd):
    return (jax.nn.silu(h @ wg) * (h @ wu)) @ wd


def retention(q, k, v, g):
    b, s, _ = q.shape
    h, d, c = RET_HEADS, RET_HEAD_DIM, RET_CHUNK
    nc = s // c
    q = rotary(q.reshape(b, s, h, d), d, RET_THETA)
    k = rotary(k.reshape(b, s, h, d), d, RET_THETA) * (d ** -0.5)
    v = v.reshape(b, s, h, d)
    to_chunks = lambda t: t.astype(jnp.float32).reshape(b, nc, c, h, d).transpose(1, 0, 3, 2, 4)
    log_gamma = jnp.log(1.0 - jnp.power(2.0, -5.0 - jnp.arange(h, dtype=jnp.float32)))
    pos = jnp.arange(c, dtype=jnp.float32)
    rel = pos[:, None] - pos[None, :]
    intra_decay = jnp.exp(jnp.maximum(rel, 0.0)[None] * log_gamma[:, None, None]) * (rel >= 0)[None]
    q_decay = jnp.exp((pos + 1.0)[None, :] * log_gamma[:, None])[..., None]
    k_decay = jnp.exp((c - 1.0 - pos)[None, :] * log_gamma[:, None])[..., None]
    chunk_decay = jnp.exp(c * log_gamma)[:, None, None]

    def step(state, qkv):
        qc, kc, vc = qkv
        scores = jnp.einsum('bhtd,bhsd->bhts', qc, kc) * intra_decay
        out = jnp.einsum('bhts,bhse->bhte', scores, vc) + jnp.einsum('bhtd,bhde->bhte', qc * q_decay, state)
        state = state * chunk_decay + jnp.einsum('bhsd,bhse->bhde', kc * k_decay, vc)
        return state, out

    state0 = jnp.zeros((b, h, d, d), jnp.float32)
    _, o = lax.scan(step, state0, (to_chunks(q), to_chunks(k), to_chunks(v)))
    o = rms_norm(o.transpose(1, 0, 3, 2, 4).reshape(b, s, h, d))
    return (o.reshape(b, s, h * d) * jax.nn.silu(g.astype(jnp.float32))).astype(g.dtype)


def spatial_gating(u, v, w_s, b_s, v_gain):
    b, s, _ = u.shape
    gr, dg, c = SGU_GROUPS, SGU_GROUP_DIM, SGU_CHUNK
    nc = s // c
    u = jax.nn.gelu(u)
    v = rms_norm(jax.nn.gelu(v).reshape(b, s, gr, dg), v_gain.reshape(gr, dg))
    w = w_s * jnp.tril(jnp.ones((c, c), w_s.dtype))[None]
    sv = jnp.einsum('gts,bcsgd->bctgd', w, v.reshape(b, nc, c, gr, dg)) + b_s.T[None, None, :, :, None]
    return (u.reshape(b, nc, c, gr, dg) * sv).reshape(b, s, gr * dg)


def pool_mixer(z, w_pool, scale):
    b, s, _ = z.shape
    zf = z.astype(jnp.float32).reshape(b, s, len(POOL_WINDOWS), POOL_GROUP_DIM)
    cs = jnp.cumsum(zf, axis=1)
    t = jnp.arange(s)
    outs = []
    for gi, w in enumerate(POOL_WINDOWS):
        c = cs[:, :, gi]
        prev = jnp.pad(c, ((0, 0), (w, 0), (0, 0)))[:, :s]
        cnt = jnp.minimum(t + 1, w).astype(jnp.float32)
        outs.append((c - prev) / cnt[None, :, None] - zf[:, :, gi])
    y = jnp.stack(outs, axis=2).astype(z.dtype)
    y = jnp.einsum('bsgc,gcd->bsgd', y, w_pool).reshape(b, s, -1)
    return y * scale


def moba(q, k, v):
    b, s, _ = q.shape
    h, hd, bs, qc_len = MOBA_HEADS, MOBA_HEAD_DIM, MOBA_BLOCK, MOBA_QUERY_CHUNK
    nb = -(-s // bs)
    s_pad = nb * bs
    nq = s // qc_len
    n_sel = min(MOBA_TOPK, nb)
    q = rotary(q.reshape(b, s, h, hd), ROT_DIM, ROPE_THETA).transpose(0, 2, 1, 3)
    k = rotary(k.reshape(b, s, h, hd), ROT_DIM, ROPE_THETA).transpose(0, 2, 1, 3)
    v = v.reshape(b, s, h, hd).transpose(0, 2, 1, 3)
    pad = ((0, 0), (0, 0), (0, s_pad - s), (0, 0))
    kb = jnp.pad(k, pad).reshape(b, h, nb, bs, hd)
    vb = jnp.pad(v, pad).reshape(b, h, nb, bs, hd)
    k_mean = jnp.mean(kb.astype(jnp.float32), axis=3)
    t = jnp.arange(s)
    q_block = t // bs
    past = jnp.arange(nb)[None, :] < q_block[:, None]
    gate = jnp.einsum('bhsd,bhnd->bhsn', q.astype(jnp.float32), k_mean)
    gate = jnp.where(past, gate, NEG)
    _, top = lax.top_k(gate, n_sel)
    own = jnp.broadcast_to(q_block.astype(top.dtype)[None, None, :, None], (b, h, s, 1))
    idx = jnp.concatenate([top, own], axis=-1)
    slot_ok = jnp.arange(n_sel)[None, :] < q_block[:, None]
    causal = jnp.arange(bs)[None, :] <= (t % bs)[:, None]
    mask = jnp.concatenate([jnp.broadcast_to(slot_ok[:, :, None], (s, n_sel, bs)), causal[:, None, :]], axis=1)
    qcs = q.reshape(b, h, nq, qc_len, hd).transpose(2, 0, 1, 3, 4)
    ics = idx.reshape(b, h, nq, qc_len, n_sel + 1).transpose(2, 0, 1, 3, 4)
    mcs = mask.reshape(nq, qc_len, n_sel + 1, bs)
    gather_blocks = jax.vmap(jax.vmap(lambda blocks, ids: blocks[ids]))
    scale = hd ** -0.5

    def attend(args):
        qc, ic, mc = args
        kg = gather_blocks(kb, ic)
        vg = gather_blocks(vb, ic)
        logits = jnp.einsum('bhqd,bhqnkd->bhqnk', qc, kg).astype(jnp.float32) * scale
        logits = jnp.where(mc, logits, NEG)
        pr = jax.nn.softmax(logits.reshape(b, h, qc_len, -1), axis=-1).astype(vg.dtype)
        return jnp.einsum('bhqk,bhqkd->bhqd', pr, vg.reshape(b, h, qc_len, -1, hd))

    o = lax.map(attend, (qcs, ics, mcs))
    return o.transpose(1, 0, 3, 2, 4).reshape(b, s, h * hd)


def even_mixer(h, w_in, w_out, sgu_w, sgu_b, sgu_gain):
    z = h @ w_in
    q, k, v, g, u, vs = jnp.split(z, 6, axis=-1)
    y = jnp.concatenate([retention(q, k, v, g), spatial_gating(u, vs, sgu_w, sgu_b, sgu_gain)], axis=-1)
    return y @ w_out


def odd_mixer(h, w_in, w_out, pool_w, pool_scale):
    z = h @ w_in
    pz, q, k, v = jnp.split(z, 4, axis=-1)
    y = jnp.concatenate([pool_mixer(pz, pool_w, pool_scale), moba(q, k, v)], axis=-1)
    return y @ w_out


def setup_inputs(seed: int = 0) -> dict:
    key = jax.random.key(seed)
    ks = jax.random.split(key, 17)
    nrm = jax.random.normal
    f32 = jnp.float32
    return {
        "x": nrm(ks[0], (BATCH, SEQ, D_MODEL), f32),
        "p": nrm(ks[1], (DEPTH, BATCH, SEQ, PLE_DIM), f32),
        "norm_gains": 1.0 + 0.1 * nrm(ks[2], (DEPTH, N_NORMS, D_MODEL), f32),
        "w_ffn_gate": nrm(ks[3], (DEPTH, 2, D_MODEL, D_FF), f32) * D_MODEL ** -0.5,
        "w_ffn_up": nrm(ks[4], (DEPTH, 2, D_MODEL, D_FF), f32) * D_MODEL ** -0.5,
        "w_ffn_down": nrm(ks[5], (DEPTH, 2, D_FF, D_MODEL), f32) * D_FF ** -0.5,
        "w_in_even": nrm(ks[6], (N_EVEN, D_MODEL, 6 * MIX_HALF), f32) * D_MODEL ** -0.5,
        "w_out_even": nrm(ks[7], (N_EVEN, 2 * MIX_HALF, D_MODEL), f32) * (2 * MIX_HALF) ** -0.5,
        "sgu_w": nrm(ks[8], (N_EVEN, SGU_GROUPS, SGU_CHUNK, SGU_CHUNK), f32) * SGU_CHUNK ** -0.5,
        "sgu_b": 1.0 + 0.1 * nrm(ks[9], (N_EVEN, SGU_GROUPS, SGU_CHUNK), f32),
        "sgu_gain": 1.0 + 0.1 * nrm(ks[10], (N_EVEN, MIX_HALF), f32),
        "w_in_odd": nrm(ks[11], (N_ODD, D_MODEL, 4 * MIX_HALF), f32) * D_MODEL ** -0.5,
        "w_out_odd": nrm(ks[12], (N_ODD, 2 * MIX_HALF, D_MODEL), f32) * (2 * MIX_HALF) ** -0.5,
        "pool_w": nrm(ks[13], (N_ODD, len(POOL_WINDOWS), POOL_GROUP_DIM, POOL_GROUP_DIM), f32) * POOL_GROUP_DIM ** -0.5,
        "pool_scale": 1.0 + 0.1 * nrm(ks[14], (N_ODD, MIX_HALF), f32),
        "w_ple_gate": nrm(ks[15], (DEPTH, D_MODEL, D_MODEL), f32) * D_MODEL ** -0.5,
        "w_ple_proj": nrm(ks[16], (DEPTH, PLE_DIM, D_MODEL), f32) * PLE_DIM ** -0.5,
    }


def reference(x, p, norm_gains, w_ffn_gate, w_ffn_up, w_ffn_down, w_in_even, w_out_even, sgu_w, sgu_b,
              sgu_gain, w_in_odd, w_out_odd, pool_w, pool_scale, w_ple_gate, w_ple_proj):
    for i in range(DEPTH):
        ng = norm_gains[i]
        h = rms_norm(x, ng[0])
        x = x + 0.5 * rms_norm(swiglu(h, w_ffn_gate[i, 0], w_ffn_up[i, 0], w_ffn_down[i, 0]), ng[1])
        h = rms_norm(x, ng[2])
        if i % 2 == 0:
            j = i // 2
            m = even_mixer(h, w_in_even[j], w_out_even[j], sgu_w[j], sgu_b[j], sgu_gain[j])
        else:
            j = i // 2
            m = odd_mixer(h, w_in_odd[j], w_out_odd[j], pool_w[j], pool_scale[j])
        x = x + rms_norm(m, ng[3])
        h = rms_norm(x, ng[4])
        x = x + 0.5 * rms_norm(swiglu(h, w_ffn_gate[i, 1], w_ffn_up[i, 1], w_ffn_down[i, 1]), ng[5])
        gate = jax.nn.sigmoid(rms_norm(x, ng[6]) @ w_ple_gate[i])
        x = x + rms_norm(gate * (p[i] @ w_ple_proj[i]), ng[7])
    return x
```

```python
import functools

import numpy as np
import jax
import jax.numpy as jnp
from jax import lax
from jax.experimental import pallas as pl
from jax.experimental.pallas import tpu as pltpu

F32 = jnp.float32
BF16 = jnp.bfloat16

RET_HEADS = 4
RET_CHUNK = 128
RET_THETA = 10000.0
SGU_GROUPS = 4
SGU_CHUNK = 128
POOL_WINDOWS = (2, 4, 8, 16)
MOBA_HEADS = 8
MOBA_BLOCK = 256
MOBA_TOPK = 3
ROPE_THETA = 500000.0
EPS = 1e-6
NEG = -1e30

V7X_VMEM_BYTES = 64 * 1024 * 1024
VMEM_REQUEST_CAP = V7X_VMEM_BYTES - 8 * 1024 * 1024


def _params(semantics, vmem_bytes):
    return pltpu.CompilerParams(dimension_semantics=semantics,
                                vmem_limit_bytes=int(min(vmem_bytes, VMEM_REQUEST_CAP)))


def _rms(x, gain=None):
    y = x * lax.rsqrt(jnp.mean(x * x, axis=-1, keepdims=True) + EPS)
    return y if gain is None else y * gain


def _dot(a, b):
    return jnp.dot(a, b, preferred_element_type=F32)


def _dot_nt(a, b, precision=None):
    return lax.dot_general(a, b, (((1,), (1,)), ((), ())), precision=precision, preferred_element_type=F32)


def _dot_tn(a, b):
    return lax.dot_general(a, b, (((0,), (0,)), ((), ())), preferred_element_type=F32)


def _ffn_kernel(x_ref, gin_ref, wg_ref, wu_ref, wd_ref, gout_ref, o_ref, h_ref, acc_ref):
    k = pl.program_id(1)

    @pl.when(k == 0)
    def _():
        h_ref[...] = _rms(x_ref[...], gin_ref[...]).astype(BF16)
        acc_ref[...] = jnp.zeros_like(acc_ref)

    h = h_ref[...]
    a = (jax.nn.silu(_dot(h, wg_ref[...])) * _dot(h, wu_ref[...])).astype(BF16)
    acc_ref[...] += _dot(a, wd_ref[...])

    @pl.when(k == pl.num_programs(1) - 1)
    def _():
        o_ref[...] = x_ref[...] + 0.5 * _rms(acc_ref[...], gout_ref[...])


def _ffn(x, gains, wg, wu, wd, layer, which, n_in, n_out):
    t, d = x.shape
    f = wg.shape[-1]
    tm = min(512, t)
    fc = min(512, f)
    gain_spec = lambda n: pl.BlockSpec((None, None, 1, d), lambda i, k: (layer, n, 0, 0))
    vmem = (4 * tm * d * 4
            + 2 * 3 * d * fc * 2
            + tm * d * (2 + 4)
            + 4 * tm * fc * 4
            + 2 * tm * d * 4)
    return pl.pallas_call(
        _ffn_kernel,
        grid=(t // tm, f // fc),
        in_specs=[
            pl.BlockSpec((tm, d), lambda i, k: (i, 0)),
            gain_spec(n_in),
            pl.BlockSpec((None, None, d, fc), lambda i, k: (layer, which, 0, k)),
            pl.BlockSpec((None, None, d, fc), lambda i, k: (layer, which, 0, k)),
            pl.BlockSpec((None, None, fc, d), lambda i, k: (layer, which, k, 0)),
            gain_spec(n_out),
        ],
        out_specs=pl.BlockSpec((tm, d), lambda i, k: (i, 0)),
        out_shape=jax.ShapeDtypeStruct((t, d), F32),
        scratch_shapes=[pltpu.VMEM((tm, d), BF16), pltpu.VMEM((tm, d), F32)],
        compiler_params=_params(("parallel", "arbitrary"), vmem),
        name="ffn",
    )(x, gains, wg, wu, wd, gains)


def _norm_matmul_kernel(x_ref, g_ref, w_ref, o_ref, h_ref):
    @pl.when(pl.program_id(1) == 0)
    def _():
        h_ref[...] = _rms(x_ref[...], g_ref[...]).astype(BF16)

    o_ref[...] = _dot(h_ref[...], w_ref[...])


def _norm_matmul(x, gains, w, layer, n_gain, j):
    t, d = x.shape
    n = w.shape[-1]
    tm = min(512, t)
    tn = min(1024, n)
    vmem = 2 * tm * d * 4 + 2 * d * tn * 2 + 2 * tm * tn * 4 + tm * d * 2 + 2 * tm * d * 4
    return pl.pallas_call(
        _norm_matmul_kernel,
        grid=(t // tm, n // tn),
        in_specs=[
            pl.BlockSpec((tm, d), lambda i, k: (i, 0)),
            pl.BlockSpec((None, None, 1, d), lambda i, k: (layer, n_gain, 0, 0)),
            pl.BlockSpec((None, d, tn), lambda i, k: (j, 0, k)),
        ],
        out_specs=pl.BlockSpec((tm, tn), lambda i, k: (i, k)),
        out_shape=jax.ShapeDtypeStruct((t, n), F32),
        scratch_shapes=[pltpu.VMEM((tm, d), BF16)],
        compiler_params=_params(("parallel", "arbitrary"), vmem),
        name="norm_in_proj",
    )(x, gains, w)


def _out_proj_kernel(ya_ref, yb_ref, x_ref, wa_ref, wb_ref, g_ref, o_ref):
    m = _dot(ya_ref[...], wa_ref[...]) + _dot(yb_ref[...], wb_ref[...])
    o_ref[...] = x_ref[...] + _rms(m, g_ref[...])


def _out_proj(ya, yb, x, gains, w, layer, n_gain, j):
    t, d = x.shape
    half = ya.shape[-1]
    tm = min(512, t)
    vmem = 2 * 2 * tm * half * 2 + 4 * tm * d * 4 + 2 * 2 * half * d * 2 + 3 * tm * d * 4
    return pl.pallas_call(
        _out_proj_kernel,
        grid=(t // tm,),
        in_specs=[
            pl.BlockSpec((tm, half), lambda i: (i, 0)),
            pl.BlockSpec((tm, half), lambda i: (i, 0)),
            pl.BlockSpec((tm, d), lambda i: (i, 0)),
            pl.BlockSpec((None, half, d), lambda i: (j, 0, 0)),
            pl.BlockSpec((None, half, d), lambda i: (j, 1, 0)),
            pl.BlockSpec((None, None, 1, d), lambda i: (layer, n_gain, 0, 0)),
        ],
        out_specs=pl.BlockSpec((tm, d), lambda i: (i, 0)),
        out_shape=jax.ShapeDtypeStruct((t, d), F32),
        compiler_params=_params(("parallel",), vmem),
        name="out_proj",
    )(ya, yb, x, w, w, gains)


def _ple_kernel(x_ref, p_ref, gin_ref, wgate_ref, wproj_ref, gout_ref, o_ref):
    x = x_ref[...]
    gate = jax.nn.sigmoid(_dot(_rms(x, gin_ref[...]).astype(BF16), wgate_ref[...]))
    emb = _dot(p_ref[...].astype(BF16), wproj_ref[...])
    o_ref[...] = x + _rms(gate * emb, gout_ref[...])


def _ple(x, p, gains, wgate, wproj, layer):
    t, d = x.shape
    e = p.shape[-1]
    tm = min(512, t)
    vmem = 4 * tm * d * 4 + 2 * tm * e * 4 + 2 * d * d * 2 + 2 * e * d * 2 + 4 * tm * d * 4
    return pl.pallas_call(
        _ple_kernel,
        grid=(t // tm,),
        in_specs=[
            pl.BlockSpec((tm, d), lambda i: (i, 0)),
            pl.BlockSpec((None, tm, e), lambda i: (layer, i, 0)),
            pl.BlockSpec((None, None, 1, d), lambda i: (layer, 6, 0, 0)),
            pl.BlockSpec((None, d, d), lambda i: (layer, 0, 0)),
            pl.BlockSpec((None, e, d), lambda i: (layer, 0, 0)),
            pl.BlockSpec((None, None, 1, d), lambda i: (layer, 7, 0, 0)),
        ],
        out_specs=pl.BlockSpec((tm, d), lambda i: (i, 0)),
        out_shape=jax.ShapeDtypeStruct((t, d), F32),
        compiler_params=_params(("parallel",), vmem),
        name="ple",
    )(x, p, gains, wgate, wproj, gains)


def _retention_tables(seq, hd):
    half = hd // 2
    inv = 1.0 / np.power(np.float32(RET_THETA), np.arange(half, dtype=np.float32) / half)
    ang = np.arange(seq, dtype=np.float32)[:, None] * inv[None, :]
    c = RET_CHUNK
    log_gamma = np.log(1.0 - np.power(2.0, -5.0 - np.arange(RET_HEADS, dtype=np.float64)))
    pos = np.arange(c, dtype=np.float64)
    rel = pos[:, None] - pos[None, :]
    intra = np.exp(np.maximum(rel, 0.0)[None] * log_gamma[:, None, None]) * (rel >= 0)[None]
    qd = np.broadcast_to(np.exp((pos + 1.0)[None, :] * log_gamma[:, None])[..., None], (RET_HEADS, c, hd))
    kd = np.broadcast_to(np.exp((c - 1.0 - pos)[None, :] * log_gamma[:, None])[..., None], (RET_HEADS, c, hd))
    cd = np.broadcast_to(np.exp(c * log_gamma)[:, None, None], (RET_HEADS, 1, hd))
    as32 = lambda a: jnp.asarray(np.ascontiguousarray(a), F32)
    return as32(np.cos(ang)), as32(np.sin(ang)), as32(intra), as32(qd), as32(kd), as32(cd)


def _even_core_kernel(q_ref, k_ref, v_ref, g_ref, u_ref, vs_ref, cos_ref, sin_ref, dec_ref, qd_ref, kd_ref,
                      cd_ref, ws_ref, bs_ref, gain_ref, oret_ref, osgu_ref, state_ref):
    hd = q_ref.shape[-1]
    half = hd // 2
    cos = cos_ref[...]
    sin = sin_ref[...]

    def rotary(t):
        t1 = t[:, :half]
        t2 = t[:, half:]
        return jnp.concatenate([t1 * cos - t2 * sin, t2 * cos + t1 * sin], axis=-1)

    @pl.when(pl.program_id(2) == 0)
    def _():
        state_ref[...] = jnp.zeros_like(state_ref)

    q = rotary(q_ref[...])
    k = rotary(k_ref[...]) * (hd ** -0.5)
    v = v_ref[...].astype(BF16)
    state = state_ref[...]
    scores = _dot_nt(q.astype(BF16), k.astype(BF16)) * dec_ref[...]
    out = _dot(scores.astype(BF16), v) + _dot((q * qd_ref[...]).astype(BF16), state.astype(BF16))
    state_ref[...] = state * cd_ref[...] + _dot_tn((k * kd_ref[...]).astype(BF16), v)
    oret_ref[...] = (_rms(out) * jax.nn.silu(g_ref[...])).astype(oret_ref.dtype)

    c = ws_ref.shape[-1]
    row = lax.broadcasted_iota(jnp.int32, (c, c), 0)
    col = lax.broadcasted_iota(jnp.int32, (c, c), 1)
    w = jnp.where(row >= col, ws_ref[...], 0.0).astype(BF16)
    vn = _rms(jax.nn.gelu(vs_ref[...]), gain_ref[...]).astype(BF16)
    sv = _dot(w, vn) + bs_ref[...]
    osgu_ref[...] = (jax.nn.gelu(u_ref[...]) * sv).astype(osgu_ref.dtype)


def _even_core(z, batch, seq, sgu_w, sgu_b, sgu_gain, j):
    t = z.shape[0]
    mix = z.shape[1] // 6
    hd = mix // RET_HEADS
    c = RET_CHUNK
    nc = seq // c
    cos, sin, intra, qd, kd, cd = _retention_tables(seq, hd)
    zspec = lambda part: pl.BlockSpec((c, hd), lambda b, h, i: (b * nc + i, part * RET_HEADS + h))
    head3 = lambda r, w: pl.BlockSpec((None, r, w), lambda b, h, i: (h, 0, 0))
    out_spec = pl.BlockSpec((c, hd), lambda b, h, i: (b * nc + i, h))
    vmem = 2 * (6 * c * hd * 4 + 2 * c * (hd // 2) * 4 + 2 * c * c * 4 + 2 * c * hd * 4 + 2 * c * hd * 2) \
        + hd * hd * 4 + 16 * c * hd * 4 + 4 * hd * hd * 4
    return pl.pallas_call(
        _even_core_kernel,
        grid=(batch, RET_HEADS, nc),
        in_specs=[zspec(0), zspec(1), zspec(2), zspec(3), zspec(4), zspec(5),
                  pl.BlockSpec((c, hd // 2), lambda b, h, i: (i, 0)),
                  pl.BlockSpec((c, hd // 2), lambda b, h, i: (i, 0)),
                  head3(c, c), head3(c, hd), head3(c, hd), head3(1, hd),
                  pl.BlockSpec((None, None, c, c), lambda b, h, i: (j, h, 0, 0)),
                  pl.BlockSpec((None, None, c, 1), lambda b, h, i: (j, h, 0, 0)),
                  pl.BlockSpec((None, None, 1, hd), lambda b, h, i: (j, h, 0, 0))],
        out_specs=[out_spec, out_spec],
        out_shape=[jax.ShapeDtypeStruct((t, mix), BF16), jax.ShapeDtypeStruct((t, mix), BF16)],
        scratch_shapes=[pltpu.VMEM((hd, hd), F32)],
        compiler_params=_params(("parallel", "parallel", "arbitrary"), vmem),
        name="even_core",
    )(z, z, z, z, z, z, cos, sin, intra, qd, kd, cd, sgu_w, sgu_b, sgu_gain)


def _pool_kernel(z_ref, w_ref, scale_ref, o_ref):
    z = z_ref[...]
    gi = pl.program_id(1)
    t = lax.broadcasted_iota(jnp.int32, z.shape, 0)

    def shifted(a, sh):
        return jnp.where(t >= sh, pltpu.roll(a, sh, 0), 0.0)

    sums = []
    s = z
    sh = 1
    for _ in POOL_WINDOWS:
        s = s + shifted(s, sh)
        sums.append(s)
        sh *= 2
    win = sums[-1]
    for idx in range(len(POOL_WINDOWS) - 2, -1, -1):
        win = jnp.where(gi == idx, sums[idx], win)
    width = jnp.left_shift(jnp.int32(POOL_WINDOWS[0]), gi)
    cnt = jnp.minimum(t + 1, width).astype(F32)
    y = win / cnt - z
    o_ref[...] = (_dot(y.astype(BF16), w_ref[...]) * scale_ref[...]).astype(o_ref.dtype)


def _pool(z, batch, seq, pool_w, pool_scale, j):
    assert all(w == POOL_WINDOWS[0] << i for i, w in enumerate(POOL_WINDOWS))
    t = z.shape[0]
    mix = z.shape[1] // 4
    ng = len(POOL_WINDOWS)
    dg = mix // ng
    vmem = 2 * (seq * dg * 4 + dg * dg * 2 + seq * dg * 2) + 10 * seq * dg * 4
    return pl.pallas_call(
        _pool_kernel,
        grid=(batch, ng),
        in_specs=[pl.BlockSpec((seq, dg), lambda b, g: (b, g)),
                  pl.BlockSpec((None, None, dg, dg), lambda b, g: (j, g, 0, 0)),
                  pl.BlockSpec((None, None, 1, dg), lambda b, g: (j, g, 0, 0))],
        out_specs=pl.BlockSpec((seq, dg), lambda b, g: (b, g)),
        out_shape=jax.ShapeDtypeStruct((t, mix), BF16),
        compiler_params=_params(("parallel", "parallel"), vmem),
        name="pool",
    )(z, pool_w, pool_scale)


def _moba_tables(seq, hd):
    rot = hd // 4
    half = rot // 2
    inv = 1.0 / np.power(np.float32(ROPE_THETA), np.arange(half, dtype=np.float32) / half)
    ang = np.arange(seq, dtype=np.float32)[:, None] * inv[None, :]
    cos = np.concatenate([np.cos(ang), np.cos(ang), np.ones((seq, hd - rot), np.float32)], axis=1)
    sin = np.concatenate([-np.sin(ang), np.sin(ang), np.zeros((seq, hd - rot), np.float32)], axis=1)
    return jnp.asarray(cos, F32), jnp.asarray(sin, F32)


def _moba_kernel(q_ref, k_ref, v_ref, cos_ref, sin_ref, o_ref):
    seq, hd = q_ref.shape
    bs = MOBA_BLOCK
    nb = seq // bs
    n_sel = min(MOBA_TOPK, nb)
    half = hd // 8
    cos = cos_ref[...]
    sin = sin_ref[...]
    lane = lax.broadcasted_iota(jnp.int32, (seq, hd), 1)

    def rotary(t):
        partner = jnp.where(lane < half, pltpu.roll(t, hd - half, 1), pltpu.roll(t, half, 1))
        return t * cos + partner * sin

    q = rotary(q_ref[...])
    k = rotary(k_ref[...])
    k_mean = jnp.mean(k.reshape(nb, bs, hd), axis=1)
    gate = _dot_nt(q, k_mean, precision=lax.Precision.HIGHEST)
    kb = k.astype(BF16)
    vb = v_ref[...].astype(BF16)
    row = lax.broadcasted_iota(jnp.int32, (bs, bs), 0)
    col = lax.broadcasted_iota(jnp.int32, (bs, bs), 1)
    scale = hd ** -0.5

    for qb in range(nb):
        qs = slice(qb * bs, (qb + 1) * bs)
        qq = q[qs].astype(BF16)
        gq = gate[qs]
        logits = []
        for jb in range(qb + 1):
            lg = _dot_nt(qq, kb[jb * bs:(jb + 1) * bs]) * scale
            if jb == qb:
                lg = jnp.where(col <= row, lg, NEG)
            elif qb > n_sel:
                mine = gq[:, jb:jb + 1]
                rank = jnp.zeros((bs, 1), F32)
                for ob in range(qb):
                    if ob != jb:
                        other = gq[:, ob:ob + 1]
                        ahead = (other >= mine) if ob < jb else (other > mine)
                        rank = rank + ahead.astype(F32)
                lg = jnp.where(rank < n_sel, lg, NEG)
            logits.append(lg)
        m = functools.reduce(jnp.maximum, [jnp.max(lg, axis=-1, keepdims=True) for lg in logits])
        probs = [jnp.exp(lg - m) for lg in logits]
        denom = functools.reduce(jnp.add, [jnp.sum(p, axis=-1, keepdims=True) for p in probs])
        acc = functools.reduce(jnp.add, [_dot(p.astype(BF16), vb[jb * bs:(jb + 1) * bs])
                                         for jb, p in enumerate(probs)])
        o_ref[qs, :] = (acc / denom).astype(o_ref.dtype)


def _moba(z, batch, seq):
    assert seq % MOBA_BLOCK == 0
    t = z.shape[0]
    mix = z.shape[1] // 4
    hd = mix // MOBA_HEADS
    cos, sin = _moba_tables(seq, hd)
    zspec = lambda part: pl.BlockSpec((seq, hd), lambda b, h: (b, part * MOBA_HEADS + h))
    table = pl.BlockSpec((seq, hd), lambda b, h: (0, 0))
    nb = seq // MOBA_BLOCK
    vmem = 2 * (3 * seq * hd * 4 + 2 * seq * hd * 4 + seq * hd * 2) + 10 * seq * hd * 4 \
        + 3 * nb * MOBA_BLOCK * MOBA_BLOCK * 4
    return pl.pallas_call(
        _moba_kernel,
        grid=(batch, MOBA_HEADS),
        in_specs=[zspec(1), zspec(2), zspec(3), table, table],
        out_specs=pl.BlockSpec((seq, hd), lambda b, h: (b, h)),
        out_shape=jax.ShapeDtypeStruct((t, mix), BF16),
        compiler_params=_params(("parallel", "parallel"), vmem),
        name="moba",
    )(z, z, z, cos, sin)


def kernel(x, p, norm_gains, w_ffn_gate, w_ffn_up, w_ffn_down, w_in_even, w_out_even, sgu_w, sgu_b, sgu_gain,
           w_in_odd, w_out_odd, pool_w, pool_scale, w_ple_gate, w_ple_proj):
    batch, seq, d = x.shape
    depth = norm_gains.shape[0]
    t = batch * seq
    assert seq % RET_CHUNK == 0 and RET_CHUNK == SGU_CHUNK and RET_HEADS == SGU_GROUPS

    gains = norm_gains.reshape(depth, norm_gains.shape[1], 1, d)
    wg, wu, wd = (w.astype(BF16) for w in (w_ffn_gate, w_ffn_up, w_ffn_down))
    w_in_e, w_out_e, w_in_o, w_out_o = (w.astype(BF16) for w in (w_in_even, w_out_even, w_in_odd, w_out_odd))
    wpg, wpp = w_ple_gate.astype(BF16), w_ple_proj.astype(BF16)
    pool_w16 = pool_w.astype(BF16)
    n_even = sgu_w.shape[0]
    sgu_b4 = sgu_b.reshape(n_even, SGU_GROUPS, SGU_CHUNK, 1)
    sgu_gain4 = sgu_gain.reshape(n_even, SGU_GROUPS, 1, -1)
    pool_scale4 = pool_scale.reshape(pool_scale.shape[0], len(POOL_WINDOWS), 1, -1)
    p3 = p.reshape(depth, t, p.shape[-1])

    xf = x.reshape(t, d)
    for i in range(depth):
        j = i // 2
        xf = _ffn(xf, gains, wg, wu, wd, i, 0, 0, 1)
        if i % 2 == 0:
            z = _norm_matmul(xf, gains, w_in_e, i, 2, j)
            ya, yb = _even_core(z, batch, seq, sgu_w, sgu_b4, sgu_gain4, j)
            xf = _out_proj(ya, yb, xf, gains, w_out_e, i, 3, j)
        else:
            z = _norm_matmul(xf, gains, w_in_o, i, 2, j)
            ya = _pool(z, batch, seq, pool_w16, pool_scale4, j)
            yb = _moba(z, batch, seq)
            xf = _out_proj(ya, yb, xf, gains, w_out_o, i, 3, j)
        xf = _ffn(xf, gains, wg, wu, wd, i, 1, 4, 5)
        xf = _ple(xf, p3, gains, wpg, wpp, i)
    return xf.reshape(batch, seq, d)
```

```python
import functools

import numpy as np
import jax
import jax.numpy as jnp
from jax import lax
from jax.experimental import pallas as pl
from jax.experimental.pallas import tpu as pltpu

F32 = jnp.float32
BF16 = jnp.bfloat16

RET_HEADS = 4
RET_CHUNK = 128
RET_THETA = 10000.0
SGU_GROUPS = 4
SGU_CHUNK = 128
POOL_WINDOWS = (2, 4, 8, 16)
MOBA_HEADS = 8
MOBA_BLOCK = 256
MOBA_TOPK = 3
ROPE_THETA = 500000.0
EPS = 1e-6
NEG = -1e30

V7X_VMEM_BYTES = 64 * 1024 * 1024
VMEM_REQUEST_CAP = V7X_VMEM_BYTES - 8 * 1024 * 1024

FFN_TOKENS = 1024
FFN_CHUNK = 512
FFN_ROWS = 512
PROJ_TOKENS = 1024
PROJ_COLS = 1024
ROW_TOKENS = 512


def _params(semantics, vmem_bytes):
    return pltpu.CompilerParams(dimension_semantics=semantics,
                                vmem_limit_bytes=int(min(vmem_bytes, VMEM_REQUEST_CAP)))


def _rms(x, gain=None):
    y = x * lax.rsqrt(jnp.mean(x * x, axis=-1, keepdims=True) + EPS)
    return y if gain is None else y * gain


def _dot(a, b):
    return jnp.dot(a, b, preferred_element_type=F32)


def _dot_nt(a, b, precision=None):
    return lax.dot_general(a, b, (((1,), (1,)), ((), ())), precision=precision, preferred_element_type=F32)


def _dot_tn(a, b):
    return lax.dot_general(a, b, (((0,), (0,)), ((), ())), preferred_element_type=F32)


def _ffn_kernel(x_ref, gin_ref, wg_ref, wu_ref, wd_ref, gout_ref, o_ref, h_ref, *, rows, nk):
    k = pl.program_id(1)
    tm = x_ref.shape[0]

    def step(first, last):
        for r in range(0, tm, rows):
            rs = slice(r, r + rows)
            if first:
                h = _rms(x_ref[rs, :], gin_ref[...]).astype(BF16)
                h_ref[rs, :] = h
            else:
                h = h_ref[rs, :]
            a = (jax.nn.silu(_dot(h, wg_ref[...])) * _dot(h, wu_ref[...])).astype(BF16)
            acc = _dot(a, wd_ref[...])
            if not first:
                acc = o_ref[rs, :] + acc
            if last:
                acc = x_ref[rs, :] + _rms(acc, 0.5 * gout_ref[...])
            o_ref[rs, :] = acc

    if nk == 1:
        step(True, True)
    else:
        pl.when(k == 0)(lambda: step(True, False))
        pl.when(jnp.logical_and(k > 0, k < nk - 1))(lambda: step(False, False))
        pl.when(k == nk - 1)(lambda: step(False, True))


def _ffn(x, gains, wg, wu, wd, layer, which, n_in, n_out):
    t, d = x.shape
    f = wg.shape[-1]
    tm = min(FFN_TOKENS, t)
    fc = min(FFN_CHUNK, f)
    rows = min(FFN_ROWS, tm)
    gain_spec = lambda n: pl.BlockSpec((None, None, 1, d), lambda i, k: (layer, n, 0, 0))
    vmem = (4 * tm * d * 4
            + 2 * 3 * d * fc * 2
            + tm * d * 2
            + 3 * rows * fc * 4
            + 2 * rows * d * 4)
    return pl.pallas_call(
        functools.partial(_ffn_kernel, rows=rows, nk=f // fc),
        grid=(t // tm, f // fc),
        in_specs=[
            pl.BlockSpec((tm, d), lambda i, k: (i, 0)),
            gain_spec(n_in),
            pl.BlockSpec((None, None, d, fc), lambda i, k: (layer, which, 0, k)),
            pl.BlockSpec((None, None, d, fc), lambda i, k: (layer, which, 0, k)),
            pl.BlockSpec((None, None, fc, d), lambda i, k: (layer, which, k, 0)),
            gain_spec(n_out),
        ],
        out_specs=pl.BlockSpec((tm, d), lambda i, k: (i, 0)),
        out_shape=jax.ShapeDtypeStruct((t, d), F32),
        scratch_shapes=[pltpu.VMEM((tm, d), BF16)],
        compiler_params=_params(("parallel", "arbitrary"), vmem),
        name="ffn",
    )(x, gains, wg, wu, wd, gains)


def _norm_matmul_kernel(x_ref, g_ref, w_ref, o_ref, h_ref, *, rows):
    def step(first):
        for r in range(0, x_ref.shape[0], rows):
            rs = slice(r, r + rows)
            if first:
                h = _rms(x_ref[rs, :], g_ref[...]).astype(BF16)
                h_ref[rs, :] = h
            else:
                h = h_ref[rs, :]
            o_ref[rs, :] = _dot(h, w_ref[...])

    pl.when(pl.program_id(1) == 0)(lambda: step(True))
    pl.when(pl.program_id(1) > 0)(lambda: step(False))


def _norm_matmul(x, gains, w, layer, n_gain, j):
    t, d = x.shape
    n = w.shape[-1]
    tm = min(PROJ_TOKENS, t)
    tn = min(PROJ_COLS, n)
    vmem = 2 * tm * d * 4 + 2 * d * tn * 2 + 2 * tm * tn * 4 + tm * d * 2 + 2 * tm * d * 4
    return pl.pallas_call(
        functools.partial(_norm_matmul_kernel, rows=min(FFN_ROWS, tm)),
        grid=(t // tm, n // tn),
        in_specs=[
            pl.BlockSpec((tm, d), lambda i, k: (i, 0)),
            pl.BlockSpec((None, None, 1, d), lambda i, k: (layer, n_gain, 0, 0)),
            pl.BlockSpec((None, d, tn), lambda i, k: (j, 0, k)),
        ],
        out_specs=pl.BlockSpec((tm, tn), lambda i, k: (i, k)),
        out_shape=jax.ShapeDtypeStruct((t, n), F32),
        scratch_shapes=[pltpu.VMEM((tm, d), BF16)],
        compiler_params=_params(("parallel", "arbitrary"), vmem),
        name="norm_in_proj",
    )(x, gains, w)


def _out_proj_kernel(ya_ref, yb_ref, x_ref, wa_ref, wb_ref, g_ref, o_ref):
    m = _dot(ya_ref[...], wa_ref[...]) + _dot(yb_ref[...], wb_ref[...])
    o_ref[...] = x_ref[...] + _rms(m, g_ref[...])


def _out_proj(ya, yb, x, gains, w, layer, n_gain, j):
    t, d = x.shape
    half = ya.shape[-1]
    tm = min(ROW_TOKENS, t)
    vmem = 2 * 2 * tm * half * 2 + 4 * tm * d * 4 + 2 * 2 * half * d * 2 + 3 * tm * d * 4
    return pl.pallas_call(
        _out_proj_kernel,
        grid=(t // tm,),
        in_specs=[
            pl.BlockSpec((tm, half), lambda i: (i, 0)),
            pl.BlockSpec((tm, half), lambda i: (i, 0)),
            pl.BlockSpec((tm, d), lambda i: (i, 0)),
            pl.BlockSpec((None, half, d), lambda i: (j, 0, 0)),
            pl.BlockSpec((None, half, d), lambda i: (j, 1, 0)),
            pl.BlockSpec((None, None, 1, d), lambda i: (layer, n_gain, 0, 0)),
        ],
        out_specs=pl.BlockSpec((tm, d), lambda i: (i, 0)),
        out_shape=jax.ShapeDtypeStruct((t, d), F32),
        compiler_params=_params(("parallel",), vmem),
        name="out_proj",
    )(ya, yb, x, w, w, gains)


def _ple_kernel(x_ref, p_ref, gin_ref, wgate_ref, wproj_ref, gout_ref, o_ref):
    x = x_ref[...]
    gate = jax.nn.sigmoid(_dot(_rms(x, gin_ref[...]).astype(BF16), wgate_ref[...]))
    emb = _dot(p_ref[...].astype(BF16), wproj_ref[...])
    o_ref[...] = x + _rms(gate * emb, gout_ref[...])


def _ple(x, p, gains, wgate, wproj, layer):
    t, d = x.shape
    e = p.shape[-1]
    tm = min(ROW_TOKENS, t)
    vmem = 4 * tm * d * 4 + 2 * tm * e * 4 + 2 * d * d * 2 + 2 * e * d * 2 + 4 * tm * d * 4
    return pl.pallas_call(
        _ple_kernel,
        grid=(t // tm,),
        in_specs=[
            pl.BlockSpec((tm, d), lambda i: (i, 0)),
            pl.BlockSpec((None, tm, e), lambda i: (layer, i, 0)),
            pl.BlockSpec((None, None, 1, d), lambda i: (layer, 6, 0, 0)),
            pl.BlockSpec((None, d, d), lambda i: (layer, 0, 0)),
            pl.BlockSpec((None, e, d), lambda i: (layer, 0, 0)),
            pl.BlockSpec((None, None, 1, d), lambda i: (layer, 7, 0, 0)),
        ],
        out_specs=pl.BlockSpec((tm, d), lambda i: (i, 0)),
        out_shape=jax.ShapeDtypeStruct((t, d), F32),
        compiler_params=_params(("parallel",), vmem),
        name="ple",
    )(x, p, gains, wgate, wproj, gains)


def _retention_tables(seq, hd):
    half = hd // 2
    inv = 1.0 / np.power(np.float32(RET_THETA), np.arange(half, dtype=np.float32) / half)
    ang = np.arange(seq, dtype=np.float32)[:, None] * inv[None, :]
    c = RET_CHUNK
    log_gamma = np.log(1.0 - np.power(2.0, -5.0 - np.arange(RET_HEADS, dtype=np.float64)))
    pos = np.arange(c, dtype=np.float64)
    rel = pos[:, None] - pos[None, :]
    intra = np.exp(np.maximum(rel, 0.0)[None] * log_gamma[:, None, None]) * (rel >= 0)[None]
    qd = np.broadcast_to(np.exp((pos + 1.0)[None, :] * log_gamma[:, None])[..., None], (RET_HEADS, c, hd))
    kd = np.broadcast_to(np.exp((c - 1.0 - pos)[None, :] * log_gamma[:, None])[..., None], (RET_HEADS, c, hd))
    cd = np.broadcast_to(np.exp(c * log_gamma)[:, None, None], (RET_HEADS, 1, hd))
    as32 = lambda a: jnp.asarray(np.ascontiguousarray(a), F32)
    return as32(np.cos(ang)), as32(np.sin(ang)), as32(intra), as32(qd), as32(kd), as32(cd)


def _even_core_kernel(q_ref, k_ref, v_ref, g_ref, u_ref, vs_ref, cos_ref, sin_ref, dec_ref, qd_ref, kd_ref,
                      cd_ref, ws_ref, bs_ref, gain_ref, oret_ref, osgu_ref):
    seq, hd = q_ref.shape
    half = hd // 2
    c = ws_ref.shape[-1]
    dec = dec_ref[...]
    qd = qd_ref[...]
    kd = kd_ref[...]
    cd = cd_ref[...]
    row = lax.broadcasted_iota(jnp.int32, (c, c), 0)
    col = lax.broadcasted_iota(jnp.int32, (c, c), 1)
    w = jnp.where(row >= col, ws_ref[...], 0.0).astype(BF16)
    bias = bs_ref[...]
    gain = gain_ref[...]

    state = jnp.zeros((hd, hd), F32)
    for i in range(seq // c):
        rs = slice(i * c, (i + 1) * c)
        cos = cos_ref[rs, :]
        sin = sin_ref[rs, :]

        def rotary(t):
            t1 = t[:, :half]
            t2 = t[:, half:]
            return jnp.concatenate([t1 * cos - t2 * sin, t2 * cos + t1 * sin], axis=-1)

        q = rotary(q_ref[rs, :])
        k = rotary(k_ref[rs, :]) * (hd ** -0.5)
        v = v_ref[rs, :].astype(BF16)
        scores = _dot_nt(q.astype(BF16), k.astype(BF16)) * dec
        out = _dot(scores.astype(BF16), v) + _dot((q * qd).astype(BF16), state.astype(BF16))
        state = state * cd + _dot_tn((k * kd).astype(BF16), v)
        oret_ref[rs, :] = (_rms(out) * jax.nn.silu(g_ref[rs, :])).astype(oret_ref.dtype)

        vn = _rms(jax.nn.gelu(vs_ref[rs, :]), gain).astype(BF16)
        sv = _dot(w, vn) + bias
        osgu_ref[rs, :] = (jax.nn.gelu(u_ref[rs, :]) * sv).astype(osgu_ref.dtype)


def _even_core(z, batch, seq, sgu_w, sgu_b, sgu_gain, j):
    t = z.shape[0]
    mix = z.shape[1] // 6
    hd = mix // RET_HEADS
    c = RET_CHUNK
    cos, sin, intra, qd, kd, cd = _retention_tables(seq, hd)
    zspec = lambda part: pl.BlockSpec((seq, hd), lambda b, h: (b, part * RET_HEADS + h))
    table = pl.BlockSpec((seq, hd // 2), lambda b, h: (0, 0))
    head3 = lambda r, w: pl.BlockSpec((None, r, w), lambda b, h: (h, 0, 0))
    param = lambda r, w: pl.BlockSpec((None, None, r, w), lambda b, h: (j, h, 0, 0))
    out_spec = pl.BlockSpec((seq, hd), lambda b, h: (b, h))
    vmem = 2 * (6 * seq * hd * 4 + 2 * seq * (hd // 2) * 4 + 2 * seq * hd * 2) + 32 * c * hd * 4 + 16 * hd * hd * 4
    return pl.pallas_call(
        _even_core_kernel,
        grid=(batch, RET_HEADS),
        in_specs=[zspec(0), zspec(1), zspec(2), zspec(3), zspec(4), zspec(5), table, table,
                  head3(c, c), head3(c, hd), head3(c, hd), head3(1, hd),
                  param(c, c), param(c, 1), param(1, hd)],
        out_specs=[out_spec, out_spec],
        out_shape=[jax.ShapeDtypeStruct((t, mix), BF16), jax.ShapeDtypeStruct((t, mix), BF16)],
        compiler_params=_params(("parallel", "parallel"), vmem),
        name="even_core",
    )(z, z, z, z, z, z, cos, sin, intra, qd, kd, cd, sgu_w, sgu_b, sgu_gain)


def _pool_kernel(z_ref, w_ref, scale_ref, o_ref):
    z = z_ref[...]
    gi = pl.program_id(1)
    t = lax.broadcasted_iota(jnp.int32, z.shape, 0)

    def shifted(a, sh):
        return jnp.where(t >= sh, pltpu.roll(a, sh, 0), 0.0)

    sums = []
    s = z
    sh = 1
    for _ in POOL_WINDOWS:
        s = s + shifted(s, sh)
        sums.append(s)
        sh *= 2
    win = sums[-1]
    for idx in range(len(POOL_WINDOWS) - 2, -1, -1):
        win = jnp.where(gi == idx, sums[idx], win)
    width = jnp.left_shift(jnp.int32(POOL_WINDOWS[0]), gi)
    cnt = jnp.minimum(t + 1, width).astype(F32)
    y = win / cnt - z
    o_ref[...] = (_dot(y.astype(BF16), w_ref[...]) * scale_ref[...]).astype(o_ref.dtype)


def _pool(z, batch, seq, pool_w, pool_scale, j):
    assert all(w == POOL_WINDOWS[0] << i for i, w in enumerate(POOL_WINDOWS))
    t = z.shape[0]
    mix = z.shape[1] // 4
    ng = len(POOL_WINDOWS)
    dg = mix // ng
    vmem = 2 * (seq * dg * 4 + dg * dg * 2 + seq * dg * 2) + 10 * seq * dg * 4
    return pl.pallas_call(
        _pool_kernel,
        grid=(batch, ng),
        in_specs=[pl.BlockSpec((seq, dg), lambda b, g: (b, g)),
                  pl.BlockSpec((None, None, dg, dg), lambda b, g: (j, g, 0, 0)),
                  pl.BlockSpec((None, None, 1, dg), lambda b, g: (j, g, 0, 0))],
        out_specs=pl.BlockSpec((seq, dg), lambda b, g: (b, g)),
        out_shape=jax.ShapeDtypeStruct((t, mix), BF16),
        compiler_params=_params(("parallel", "parallel"), vmem),
        name="pool",
    )(z, pool_w, pool_scale)


def _moba_tables(seq, hd):
    rot = hd // 4
    half = rot // 2
    inv = 1.0 / np.power(np.float32(ROPE_THETA), np.arange(half, dtype=np.float32) / half)
    ang = np.arange(seq, dtype=np.float32)[:, None] * inv[None, :]
    cos = np.concatenate([np.cos(ang), np.cos(ang), np.ones((seq, hd - rot), np.float32)], axis=1)
    sin = np.concatenate([-np.sin(ang), np.sin(ang), np.zeros((seq, hd - rot), np.float32)], axis=1)
    return jnp.asarray(cos, F32), jnp.asarray(sin, F32)


def _moba_kernel(q_ref, k_ref, v_ref, cos_ref, sin_ref, o_ref):
    seq, hd = q_ref.shape
    bs = MOBA_BLOCK
    nb = seq // bs
    n_sel = min(MOBA_TOPK, nb)
    half = hd // 8
    cos = cos_ref[...]
    sin = sin_ref[...]
    lane = lax.broadcasted_iota(jnp.int32, (seq, hd), 1)

    def rotary(t):
        partner = jnp.where(lane < half, pltpu.roll(t, hd - half, 1), pltpu.roll(t, half, 1))
        return t * cos + partner * sin

    q = rotary(q_ref[...])
    k = rotary(k_ref[...])
    k_mean = jnp.mean(k.reshape(nb, bs, hd), axis=1)
    qt = q.T
    gate = jnp.dot(k_mean, qt, precision=lax.Precision.HIGHEST, preferred_element_type=F32)
    qtb = qt.astype(BF16)
    kb = k.astype(BF16)
    vtb = v_ref[...].T.astype(BF16)
    key = lax.broadcasted_iota(jnp.int32, (bs, bs), 0)
    qry = lax.broadcasted_iota(jnp.int32, (bs, bs), 1)
    scale = hd ** -0.5

    for qb in range(nb):
        qs = slice(qb * bs, (qb + 1) * bs)
        qq = qtb[:, qs]
        gq = gate[:, qs]
        logits = []
        for jb in range(qb + 1):
            lg = _dot(kb[jb * bs:(jb + 1) * bs], qq) * scale
            if jb == qb:
                lg = jnp.where(key <= qry, lg, NEG)
            elif qb > n_sel:
                mine = gq[jb:jb + 1]
                rank = jnp.zeros((1, bs), F32)
                for ob in range(qb):
                    if ob != jb:
                        other = gq[ob:ob + 1]
                        ahead = (other >= mine) if ob < jb else (other > mine)
                        rank = rank + ahead.astype(F32)
                lg = lg + jnp.where(rank < n_sel, 0.0, NEG)
            logits.append(lg)
        m = functools.reduce(jnp.maximum, [jnp.max(lg, axis=0, keepdims=True) for lg in logits])
        probs = [jnp.exp(lg - m) for lg in logits]
        denom = functools.reduce(jnp.add, [jnp.sum(p, axis=0, keepdims=True) for p in probs])
        acc = functools.reduce(jnp.add, [_dot(vtb[:, jb * bs:(jb + 1) * bs], p.astype(BF16))
                                         for jb, p in enumerate(probs)])
        o_ref[qs, :] = (acc / denom).T.astype(o_ref.dtype)


def _moba(z, batch, seq):
    assert seq % MOBA_BLOCK == 0
    t = z.shape[0]
    mix = z.shape[1] // 4
    hd = mix // MOBA_HEADS
    cos, sin = _moba_tables(seq, hd)
    zspec = lambda part: pl.BlockSpec((seq, hd), lambda b, h: (b, part * MOBA_HEADS + h))
    table = pl.BlockSpec((seq, hd), lambda b, h: (0, 0))
    nb = seq // MOBA_BLOCK
    vmem = 2 * (3 * seq * hd * 4 + 2 * seq * hd * 4 + seq * hd * 2) + 10 * seq * hd * 4 \
        + 3 * nb * MOBA_BLOCK * MOBA_BLOCK * 4
    return pl.pallas_call(
        _moba_kernel,
        grid=(batch, MOBA_HEADS),
        in_specs=[zspec(1), zspec(2), zspec(3), table, table],
        out_specs=pl.BlockSpec((seq, hd), lambda b, h: (b, h)),
        out_shape=jax.ShapeDtypeStruct((t, mix), BF16),
        compiler_params=_params(("parallel", "parallel"), vmem),
        name="moba",
    )(z, z, z, cos, sin)


def kernel(x, p, norm_gains, w_ffn_gate, w_ffn_up, w_ffn_down, w_in_even, w_out_even, sgu_w, sgu_b, sgu_gain,
           w_in_odd, w_out_odd, pool_w, pool_scale, w_ple_gate, w_ple_proj):
    batch, seq, d = x.shape
    depth = norm_gains.shape[0]
    t = batch * seq
    assert seq % RET_CHUNK == 0 and RET_CHUNK == SGU_CHUNK and RET_HEADS == SGU_GROUPS

    gains = norm_gains.reshape(depth, norm_gains.shape[1], 1, d)
    wg, wu, wd = (w.astype(BF16) for w in (w_ffn_gate, w_ffn_up, w_ffn_down))
    w_in_e, w_out_e, w_in_o, w_out_o = (w.astype(BF16) for w in (w_in_even, w_out_even, w_in_odd, w_out_odd))
    wpg, wpp = w_ple_gate.astype(BF16), w_ple_proj.astype(BF16)
    pool_w16 = pool_w.astype(BF16)
    n_even = sgu_w.shape[0]
    sgu_b4 = sgu_b.reshape(n_even, SGU_GROUPS, SGU_CHUNK, 1)
    sgu_gain4 = sgu_gain.reshape(n_even, SGU_GROUPS, 1, -1)
    pool_scale4 = pool_scale.reshape(pool_scale.shape[0], len(POOL_WINDOWS), 1, -1)
    p3 = p.reshape(depth, t, p.shape[-1])

    xf = x.reshape(t, d)
    for i in range(depth):
        j = i // 2
        xf = _ffn(xf, gains, wg, wu, wd, i, 0, 0, 1)
        if i % 2 == 0:
            z = _norm_matmul(xf, gains, w_in_e, i, 2, j)
            ya, yb = _even_core(z, batch, seq, sgu_w, sgu_b4, sgu_gain4, j)
            xf = _out_proj(ya, yb, xf, gains, w_out_e, i, 3, j)
        else:
            z = _norm_matmul(xf, gains, w_in_o, i, 2, j)
            ya = _pool(z, batch, seq, pool_w16, pool_scale4, j)
            yb = _moba(z, batch, seq)
            xf = _out_proj(ya, yb, xf, gains, w_out_o, i, 3, j)
        xf = _ffn(xf, gains, wg, wu, wd, i, 1, 4, 5)
        xf = _ple(xf, p3, gains, wpg, wpp, i)
    return xf.reshape(batch, seq, d)
```

```python
import functools

import numpy as np
import jax
import jax.numpy as jnp
from jax import lax
from jax.experimental import pallas as pl
from jax.experimental.pallas import tpu as pltpu

F32 = jnp.float32
BF16 = jnp.bfloat16

RET_HEADS = 4
RET_CHUNK = 128
RET_THETA = 10000.0
SGU_GROUPS = 4
SGU_CHUNK = 128
POOL_WINDOWS = (2, 4, 8, 16)
MOBA_HEADS = 8
MOBA_BLOCK = 256
MOBA_TOPK = 3
ROPE_THETA = 500000.0
EPS = 1e-6
NEG = -1e30
LOG2_E = 1.4426950408889634
DENOM_ROWS = 16

V7X_VMEM_BYTES = 64 * 1024 * 1024
VMEM_REQUEST_CAP = V7X_VMEM_BYTES - 8 * 1024 * 1024

FFN_TOKENS = 1024
FFN_CHUNK = 512
FFN_ROWS = 512
PROJ_TOKENS = 1024
PROJ_COLS = 1024
ROW_TOKENS = 512


def _params(semantics, vmem_bytes):
    return pltpu.CompilerParams(dimension_semantics=semantics,
                                vmem_limit_bytes=int(min(vmem_bytes, VMEM_REQUEST_CAP)))


def _rms(x, gain=None):
    y = x * lax.rsqrt(jnp.mean(x * x, axis=-1, keepdims=True) + EPS)
    return y if gain is None else y * gain


def _dot(a, b):
    return jnp.dot(a, b, preferred_element_type=F32)


def _dot_nt(a, b, precision=None):
    return lax.dot_general(a, b, (((1,), (1,)), ((), ())), precision=precision, preferred_element_type=F32)


def _dot_tn(a, b):
    return lax.dot_general(a, b, (((0,), (0,)), ((), ())), preferred_element_type=F32)


def _ffn_kernel(x_ref, gin_ref, wg_ref, wu_ref, wd_ref, gout_ref, o_ref, h_ref, *, rows, nk):
    k = pl.program_id(1)
    tm = x_ref.shape[0]

    def step(first, last):
        for r in range(0, tm, rows):
            rs = slice(r, r + rows)
            if first:
                h = _rms(x_ref[rs, :], gin_ref[...]).astype(BF16)
                h_ref[rs, :] = h
            else:
                h = h_ref[rs, :]
            a = (jax.nn.silu(_dot(h, wg_ref[...])) * _dot(h, wu_ref[...])).astype(BF16)
            acc = _dot(a, wd_ref[...])
            if not first:
                acc = o_ref[rs, :] + acc
            if last:
                acc = x_ref[rs, :] + _rms(acc, 0.5 * gout_ref[...])
            o_ref[rs, :] = acc

    if nk == 1:
        step(True, True)
    else:
        pl.when(k == 0)(lambda: step(True, False))
        pl.when(jnp.logical_and(k > 0, k < nk - 1))(lambda: step(False, False))
        pl.when(k == nk - 1)(lambda: step(False, True))


def _ffn(x, gains, wg, wu, wd, layer, which, n_in, n_out):
    t, d = x.shape
    f = wg.shape[-1]
    tm = min(FFN_TOKENS, t)
    fc = min(FFN_CHUNK, f)
    rows = min(FFN_ROWS, tm)
    gain_spec = lambda n: pl.BlockSpec((None, None, 1, d), lambda i, k: (layer, n, 0, 0))
    vmem = (4 * tm * d * 4
            + 2 * 3 * d * fc * 2
            + tm * d * 2
            + 3 * rows * fc * 4
            + 2 * rows * d * 4)
    return pl.pallas_call(
        functools.partial(_ffn_kernel, rows=rows, nk=f // fc),
        grid=(t // tm, f // fc),
        in_specs=[
            pl.BlockSpec((tm, d), lambda i, k: (i, 0)),
            gain_spec(n_in),
            pl.BlockSpec((None, None, d, fc), lambda i, k: (layer, which, 0, k)),
            pl.BlockSpec((None, None, d, fc), lambda i, k: (layer, which, 0, k)),
            pl.BlockSpec((None, None, fc, d), lambda i, k: (layer, which, k, 0)),
            gain_spec(n_out),
        ],
        out_specs=pl.BlockSpec((tm, d), lambda i, k: (i, 0)),
        out_shape=jax.ShapeDtypeStruct((t, d), F32),
        scratch_shapes=[pltpu.VMEM((tm, d), BF16)],
        compiler_params=_params(("parallel", "arbitrary"), vmem),
        name="ffn",
    )(x, gains, wg, wu, wd, gains)


def _norm_matmul_kernel(x_ref, g_ref, w_ref, o_ref, h_ref, *, rows):
    def step(first):
        for r in range(0, x_ref.shape[0], rows):
            rs = slice(r, r + rows)
            if first:
                h = _rms(x_ref[rs, :], g_ref[...]).astype(BF16)
                h_ref[rs, :] = h
            else:
                h = h_ref[rs, :]
            o_ref[rs, :] = _dot(h, w_ref[...]).astype(o_ref.dtype)

    pl.when(pl.program_id(1) == 0)(lambda: step(True))
    pl.when(pl.program_id(1) > 0)(lambda: step(False))


def _norm_matmul(x, gains, w, layer, n_gain, j):
    t, d = x.shape
    n = w.shape[-1]
    tm = min(PROJ_TOKENS, t)
    tn = min(PROJ_COLS, n)
    vmem = 2 * tm * d * 4 + 2 * d * tn * 2 + 2 * tm * tn * 2 + tm * d * 2 + 2 * tm * d * 4 + tm * tn * 4
    return pl.pallas_call(
        functools.partial(_norm_matmul_kernel, rows=min(FFN_ROWS, tm)),
        grid=(t // tm, n // tn),
        in_specs=[
            pl.BlockSpec((tm, d), lambda i, k: (i, 0)),
            pl.BlockSpec((None, None, 1, d), lambda i, k: (layer, n_gain, 0, 0)),
            pl.BlockSpec((None, d, tn), lambda i, k: (j, 0, k)),
        ],
        out_specs=pl.BlockSpec((tm, tn), lambda i, k: (i, k)),
        out_shape=jax.ShapeDtypeStruct((t, n), BF16),
        scratch_shapes=[pltpu.VMEM((tm, d), BF16)],
        compiler_params=_params(("parallel", "arbitrary"), vmem),
        name="norm_in_proj",
    )(x, gains, w)


def _out_proj_kernel(ya_ref, yb_ref, x_ref, wa_ref, wb_ref, g_ref, o_ref):
    m = _dot(ya_ref[...], wa_ref[...]) + _dot(yb_ref[...], wb_ref[...])
    o_ref[...] = x_ref[...] + _rms(m, g_ref[...])


def _out_proj(ya, yb, x, gains, w, layer, n_gain, j):
    t, d = x.shape
    half = ya.shape[-1]
    tm = min(ROW_TOKENS, t)
    vmem = 2 * 2 * tm * half * 2 + 4 * tm * d * 4 + 2 * 2 * half * d * 2 + 3 * tm * d * 4
    return pl.pallas_call(
        _out_proj_kernel,
        grid=(t // tm,),
        in_specs=[
            pl.BlockSpec((tm, half), lambda i: (i, 0)),
            pl.BlockSpec((tm, half), lambda i: (i, 0)),
            pl.BlockSpec((tm, d), lambda i: (i, 0)),
            pl.BlockSpec((None, half, d), lambda i: (j, 0, 0)),
            pl.BlockSpec((None, half, d), lambda i: (j, 1, 0)),
            pl.BlockSpec((None, None, 1, d), lambda i: (layer, n_gain, 0, 0)),
        ],
        out_specs=pl.BlockSpec((tm, d), lambda i: (i, 0)),
        out_shape=jax.ShapeDtypeStruct((t, d), F32),
        compiler_params=_params(("parallel",), vmem),
        name="out_proj",
    )(ya, yb, x, w, w, gains)


def _ple_kernel(x_ref, p_ref, gin_ref, wgate_ref, wproj_ref, gout_ref, o_ref):
    x = x_ref[...]
    gate = jax.nn.sigmoid(_dot(_rms(x, gin_ref[...]).astype(BF16), wgate_ref[...]))
    emb = _dot(p_ref[...].astype(BF16), wproj_ref[...])
    o_ref[...] = x + _rms(gate * emb, gout_ref[...])


def _ple(x, p, gains, wgate, wproj, layer):
    t, d = x.shape
    e = p.shape[-1]
    tm = min(ROW_TOKENS, t)
    vmem = 4 * tm * d * 4 + 2 * tm * e * 4 + 2 * d * d * 2 + 2 * e * d * 2 + 4 * tm * d * 4
    return pl.pallas_call(
        _ple_kernel,
        grid=(t // tm,),
        in_specs=[
            pl.BlockSpec((tm, d), lambda i: (i, 0)),
            pl.BlockSpec((None, tm, e), lambda i: (layer, i, 0)),
            pl.BlockSpec((None, None, 1, d), lambda i: (layer, 6, 0, 0)),
            pl.BlockSpec((None, d, d), lambda i: (layer, 0, 0)),
            pl.BlockSpec((None, e, d), lambda i: (layer, 0, 0)),
            pl.BlockSpec((None, None, 1, d), lambda i: (layer, 7, 0, 0)),
        ],
        out_specs=pl.BlockSpec((tm, d), lambda i: (i, 0)),
        out_shape=jax.ShapeDtypeStruct((t, d), F32),
        compiler_params=_params(("parallel",), vmem),
        name="ple",
    )(x, p, gains, wgate, wproj, gains)


def _retention_tables(seq, hd):
    half = hd // 2
    inv = 1.0 / np.power(np.float32(RET_THETA), np.arange(half, dtype=np.float32) / half)
    ang = np.arange(seq, dtype=np.float32)[:, None] * inv[None, :]
    c = RET_CHUNK
    log_gamma = np.log(1.0 - np.power(2.0, -5.0 - np.arange(RET_HEADS, dtype=np.float64)))
    pos = np.arange(c, dtype=np.float64)
    rel = pos[:, None] - pos[None, :]
    intra = np.exp(np.maximum(rel, 0.0)[None] * log_gamma[:, None, None]) * (rel >= 0)[None]
    qd = np.broadcast_to(np.exp((pos + 1.0)[None, :] * log_gamma[:, None])[..., None], (RET_HEADS, c, hd))
    kd = np.broadcast_to(np.exp((c - 1.0 - pos)[None, :] * log_gamma[:, None])[..., None], (RET_HEADS, c, hd))
    cd = np.broadcast_to(np.exp(c * log_gamma)[:, None, None], (RET_HEADS, 1, hd))
    as32 = lambda a: jnp.asarray(np.ascontiguousarray(a), F32)
    return as32(np.cos(ang)), as32(np.sin(ang)), as32(intra), as32(qd), as32(kd), as32(cd)


def _even_core_kernel(q_ref, k_ref, v_ref, g_ref, u_ref, vs_ref, cos_ref, sin_ref, dec_ref, qd_ref, kd_ref,
                      cd_ref, ws_ref, bs_ref, gain_ref, oret_ref, osgu_ref):
    seq, hd = q_ref.shape
    half = hd // 2
    c = ws_ref.shape[-1]
    dec = dec_ref[...]
    qd = qd_ref[...]
    kd = kd_ref[...]
    cd = cd_ref[...]
    row = lax.broadcasted_iota(jnp.int32, (c, c), 0)
    col = lax.broadcasted_iota(jnp.int32, (c, c), 1)
    w = jnp.where(row >= col, ws_ref[...], 0.0).astype(BF16)
    bias = bs_ref[...]
    gain = gain_ref[...]

    state = jnp.zeros((hd, hd), F32)
    for i in range(seq // c):
        rs = slice(i * c, (i + 1) * c)
        cos = cos_ref[rs, :]
        sin = sin_ref[rs, :]

        def rotary(t):
            t1 = t[:, :half]
            t2 = t[:, half:]
            return jnp.concatenate([t1 * cos - t2 * sin, t2 * cos + t1 * sin], axis=-1)

        q = rotary(q_ref[rs, :].astype(F32))
        k = rotary(k_ref[rs, :].astype(F32)) * (hd ** -0.5)
        v = v_ref[rs, :]
        scores = _dot_nt(q.astype(BF16), k.astype(BF16)) * dec
        out = _dot(scores.astype(BF16), v) + _dot((q * qd).astype(BF16), state.astype(BF16))
        state = state * cd + _dot_tn((k * kd).astype(BF16), v)
        oret_ref[rs, :] = (_rms(out) * jax.nn.silu(g_ref[rs, :].astype(F32))).astype(oret_ref.dtype)

        vn = _rms(jax.nn.gelu(vs_ref[rs, :].astype(F32)), gain).astype(BF16)
        sv = _dot(w, vn) + bias
        osgu_ref[rs, :] = (jax.nn.gelu(u_ref[rs, :].astype(F32)) * sv).astype(osgu_ref.dtype)


def _even_core(z, batch, seq, sgu_w, sgu_b, sgu_gain, j):
    t = z.shape[0]
    mix = z.shape[1] // 6
    hd = mix // RET_HEADS
    c = RET_CHUNK
    cos, sin, intra, qd, kd, cd = _retention_tables(seq, hd)
    zspec = lambda part: pl.BlockSpec((seq, hd), lambda b, h: (b, part * RET_HEADS + h))
    table = pl.BlockSpec((seq, hd // 2), lambda b, h: (0, 0))
    head3 = lambda r, w: pl.BlockSpec((None, r, w), lambda b, h: (h, 0, 0))
    param = lambda r, w: pl.BlockSpec((None, None, r, w), lambda b, h: (j, h, 0, 0))
    out_spec = pl.BlockSpec((seq, hd), lambda b, h: (b, h))
    vmem = 2 * (6 * seq * hd * 4 + 2 * seq * (hd // 2) * 4 + 2 * seq * hd * 2) + 32 * c * hd * 4 + 16 * hd * hd * 4
    return pl.pallas_call(
        _even_core_kernel,
        grid=(batch, RET_HEADS),
        in_specs=[zspec(0), zspec(1), zspec(2), zspec(3), zspec(4), zspec(5), table, table,
                  head3(c, c), head3(c, hd), head3(c, hd), head3(1, hd),
                  param(c, c), param(c, 1), param(1, hd)],
        out_specs=[out_spec, out_spec],
        out_shape=[jax.ShapeDtypeStruct((t, mix), BF16), jax.ShapeDtypeStruct((t, mix), BF16)],
        compiler_params=_params(("parallel", "parallel"), vmem),
        name="even_core",
    )(z, z, z, z, z, z, cos, sin, intra, qd, kd, cd, sgu_w, sgu_b, sgu_gain)


def _pool_kernel(z_ref, w_ref, scale_ref, o_ref):
    z = z_ref[...].astype(F32)
    gi = pl.program_id(1)
    t = lax.broadcasted_iota(jnp.int32, z.shape, 0)

    def shifted(a, sh):
        return jnp.where(t >= sh, pltpu.roll(a, sh, 0), 0.0)

    sums = []
    s = z
    sh = 1
    for _ in POOL_WINDOWS:
        s = s + shifted(s, sh)
        sums.append(s)
        sh *= 2
    win = sums[-1]
    for idx in range(len(POOL_WINDOWS) - 2, -1, -1):
        win = jnp.where(gi == idx, sums[idx], win)
    width = jnp.left_shift(jnp.int32(POOL_WINDOWS[0]), gi)
    cnt = jnp.minimum(t + 1, width).astype(F32)
    y = win / cnt - z
    o_ref[...] = (_dot(y.astype(BF16), w_ref[...]) * scale_ref[...]).astype(o_ref.dtype)


def _pool(z, batch, seq, pool_w, pool_scale, j):
    assert all(w == POOL_WINDOWS[0] << i for i, w in enumerate(POOL_WINDOWS))
    t = z.shape[0]
    mix = z.shape[1] // 4
    ng = len(POOL_WINDOWS)
    dg = mix // ng
    vmem = 2 * (seq * dg * 4 + dg * dg * 2 + seq * dg * 2) + 10 * seq * dg * 4
    return pl.pallas_call(
        _pool_kernel,
        grid=(batch, ng),
        in_specs=[pl.BlockSpec((seq, dg), lambda b, g: (b, g)),
                  pl.BlockSpec((None, None, dg, dg), lambda b, g: (j, g, 0, 0)),
                  pl.BlockSpec((None, None, 1, dg), lambda b, g: (j, g, 0, 0))],
        out_specs=pl.BlockSpec((seq, dg), lambda b, g: (b, g)),
        out_shape=jax.ShapeDtypeStruct((t, mix), BF16),
        compiler_params=_params(("parallel", "parallel"), vmem),
        name="pool",
    )(z, pool_w, pool_scale)


def _moba_tables(seq, hd):
    rot = hd // 4
    half = rot // 2
    inv = 1.0 / np.power(np.float32(ROPE_THETA), np.arange(half, dtype=np.float32) / half)
    ang = np.arange(seq, dtype=np.float32)[:, None] * inv[None, :]
    cos = np.concatenate([np.cos(ang), np.cos(ang), np.ones((seq, hd - rot), np.float32)], axis=1)
    sin = np.concatenate([-np.sin(ang), np.sin(ang), np.zeros((seq, hd - rot), np.float32)], axis=1)
    return jnp.asarray(cos, F32), jnp.asarray(sin, F32)


def _moba_kernel(q_ref, k_ref, v_ref, cos_ref, sin_ref, o_ref):
    seq, hd = q_ref.shape
    bs = MOBA_BLOCK
    nb = seq // bs
    n_sel = min(MOBA_TOPK, nb)
    half = hd // 8
    cos = cos_ref[...]
    sin = sin_ref[...]
    lane = lax.broadcasted_iota(jnp.int32, (seq, hd), 1)

    def rotary(t):
        partner = jnp.where(lane < half, pltpu.roll(t, hd - half, 1), pltpu.roll(t, half, 1))
        return t * cos + partner * sin

    q = rotary(q_ref[...].astype(F32))
    k = rotary(k_ref[...].astype(F32))
    k_mean = jnp.mean(k.reshape(nb, bs, hd), axis=1)
    qt = q.T
    gate = jnp.dot(k_mean, qt, precision=lax.Precision.HIGHEST, preferred_element_type=F32)

    q_blk = lax.broadcasted_iota(jnp.int32, (1, seq), 1) // bs
    bias_rows = []
    for n in range(nb):
        mine = gate[n:n + 1]
        rank = jnp.zeros((1, seq), F32)
        for o in range(nb):
            if o != n:
                other = gate[o:o + 1]
                ahead = (other >= mine) if o < n else (other > mine)
                rank = rank + jnp.where(jnp.logical_and(ahead, q_blk > o), 1.0, 0.0)
        keep = jnp.logical_or(rank < n_sel, q_blk <= n)
        bias_rows.append(jnp.where(keep, 0.0, NEG))

    scale = hd ** -0.5 * LOG2_E
    q_aug = jnp.concatenate([qt * scale] + bias_rows + [jnp.zeros((hd - nb, seq), F32)], axis=0).astype(BF16)
    k_blk = lax.broadcasted_iota(jnp.int32, (seq, hd), 0) // bs
    one_hot = jnp.where(k_blk == lax.broadcasted_iota(jnp.int32, (seq, hd), 1), 1.0, 0.0)
    k_aug = jnp.concatenate([k, one_hot], axis=1).astype(BF16)
    vt = v_ref[...].astype(F32).T
    ones_rows = jnp.where(lax.broadcasted_iota(jnp.int32, (DENOM_ROWS, seq), 0) == 0, 1.0, 0.0)
    v_aug = jnp.concatenate([vt, ones_rows], axis=0).astype(BF16)
    key = lax.broadcasted_iota(jnp.int32, (bs, bs), 0)
    qry = lax.broadcasted_iota(jnp.int32, (bs, bs), 1)

    for qb in range(nb):
        qs = slice(qb * bs, (qb + 1) * bs)
        qq = q_aug[:, qs]
        logits = []
        for jb in range(qb + 1):
            lg = _dot(k_aug[jb * bs:(jb + 1) * bs], qq)
            if jb == qb:
                lg = jnp.where(key <= qry, lg, NEG)
            logits.append(lg)
        m = functools.reduce(jnp.maximum, [jnp.max(lg, axis=0, keepdims=True) for lg in logits])
        acc = functools.reduce(jnp.add, [_dot(v_aug[:, jb * bs:(jb + 1) * bs], jnp.exp2(lg - m).astype(BF16))
                                         for jb, lg in enumerate(logits)])
        o_ref[qs, :] = (acc[:hd] / acc[hd:hd + 1]).T.astype(o_ref.dtype)


def _moba(z, batch, seq):
    assert seq % MOBA_BLOCK == 0
    t = z.shape[0]
    mix = z.shape[1] // 4
    hd = mix // MOBA_HEADS
    cos, sin = _moba_tables(seq, hd)
    zspec = lambda part: pl.BlockSpec((seq, hd), lambda b, h: (b, part * MOBA_HEADS + h))
    table = pl.BlockSpec((seq, hd), lambda b, h: (0, 0))
    nb = seq // MOBA_BLOCK
    vmem = 2 * (3 * seq * hd * 4 + 2 * seq * hd * 4 + seq * hd * 2) + 10 * seq * hd * 4 \
        + 3 * nb * MOBA_BLOCK * MOBA_BLOCK * 4
    return pl.pallas_call(
        _moba_kernel,
        grid=(batch, MOBA_HEADS),
        in_specs=[zspec(1), zspec(2), zspec(3), table, table],
        out_specs=pl.BlockSpec((seq, hd), lambda b, h: (b, h)),
        out_shape=jax.ShapeDtypeStruct((t, mix), BF16),
        compiler_params=_params(("parallel", "parallel"), vmem),
        name="moba",
    )(z, z, z, cos, sin)


def kernel(x, p, norm_gains, w_ffn_gate, w_ffn_up, w_ffn_down, w_in_even, w_out_even, sgu_w, sgu_b, sgu_gain,
           w_in_odd, w_out_odd, pool_w, pool_scale, w_ple_gate, w_ple_proj):
    batch, seq, d = x.shape
    depth = norm_gains.shape[0]
    t = batch * seq
    assert seq % RET_CHUNK == 0 and RET_CHUNK == SGU_CHUNK and RET_HEADS == SGU_GROUPS

    gains = norm_gains.reshape(depth, norm_gains.shape[1], 1, d)
    wg, wu, wd = (w.astype(BF16) for w in (w_ffn_gate, w_ffn_up, w_ffn_down))
    w_in_e, w_out_e, w_in_o, w_out_o = (w.astype(BF16) for w in (w_in_even, w_out_even, w_in_odd, w_out_odd))
    wpg, wpp = w_ple_gate.astype(BF16), w_ple_proj.astype(BF16)
    pool_w16 = pool_w.astype(BF16)
    n_even = sgu_w.shape[0]
    sgu_b4 = sgu_b.reshape(n_even, SGU_GROUPS, SGU_CHUNK, 1)
    sgu_gain4 = sgu_gain.reshape(n_even, SGU_GROUPS, 1, -1)
    pool_scale4 = pool_scale.reshape(pool_scale.shape[0], len(POOL_WINDOWS), 1, -1)
    p3 = p.reshape(depth, t, p.shape[-1])

    xf = x.reshape(t, d)
    for i in range(depth):
        j = i // 2
        xf = _ffn(xf, gains, wg, wu, wd, i, 0, 0, 1)
        if i % 2 == 0:
            z = _norm_matmul(xf, gains, w_in_e, i, 2, j)
            ya, yb = _even_core(z, batch, seq, sgu_w, sgu_b4, sgu_gain4, j)
            xf = _out_proj(ya, yb, xf, gains, w_out_e, i, 3, j)
        else:
            z = _norm_matmul(xf, gains, w_in_o, i, 2, j)
            ya = _pool(z, batch, seq, pool_w16, pool_scale4, j)
            yb = _moba(z, batch, seq)
            xf = _out_proj(ya, yb, xf, gains, w_out_o, i, 3, j)
        xf = _ffn(xf, gains, wg, wu, wd, i, 1, 4, 5)
        xf = _ple(xf, p3, gains, wpg, wpp, i)
    return xf.reshape(batch, seq, d)
```

```python
import functools

import numpy as np
import jax
import jax.numpy as jnp
from jax import lax
from jax.experimental import pallas as pl
from jax.experimental.pallas import tpu as pltpu

F32 = jnp.float32
BF16 = jnp.bfloat16

RET_HEADS = 4
RET_CHUNK = 128
RET_THETA = 10000.0
SGU_GROUPS = 4
SGU_CHUNK = 128
POOL_WINDOWS = (2, 4, 8, 16)
MOBA_HEADS = 8
MOBA_BLOCK = 256
MOBA_TOPK = 3
MOBA_HEADS_PER_STEP = 4
ROPE_THETA = 500000.0
EPS = 1e-6
NEG = -1e30
LOG2_E = 1.4426950408889634
DENOM_ROWS = 16

V7X_VMEM_BYTES = 64 * 1024 * 1024
VMEM_REQUEST_CAP = V7X_VMEM_BYTES - 8 * 1024 * 1024

FFN_TOKENS = 1024
FFN_CHUNK = 512
FFN_ROWS = 512
PROJ_TOKENS = 1024
PROJ_COLS = 2048
ROW_TOKENS = 1024
ROW_GROUP = 512


def _params(semantics, vmem_bytes):
    return pltpu.CompilerParams(dimension_semantics=semantics,
                                vmem_limit_bytes=int(min(vmem_bytes, VMEM_REQUEST_CAP)))


def _rms(x, gain=None):
    y = x * lax.rsqrt(jnp.mean(x * x, axis=-1, keepdims=True) + EPS)
    return y if gain is None else y * gain


def _dot(a, b):
    return jnp.dot(a, b, preferred_element_type=F32)


def _dot_nt(a, b, precision=None):
    return lax.dot_general(a, b, (((1,), (1,)), ((), ())), precision=precision, preferred_element_type=F32)


def _dot_tn(a, b):
    return lax.dot_general(a, b, (((0,), (0,)), ((), ())), preferred_element_type=F32)


def _ffn_kernel(x_ref, gin_ref, wg_ref, wu_ref, wd_ref, gout_ref, o_ref, h_ref, *, rows, nk):
    k = pl.program_id(1)
    tm = x_ref.shape[0]

    def step(first, last):
        for r in range(0, tm, rows):
            rs = slice(r, r + rows)
            if first:
                h = _rms(x_ref[rs, :], gin_ref[...]).astype(BF16)
                h_ref[rs, :] = h
            else:
                h = h_ref[rs, :]
            a = (jax.nn.silu(_dot(h, wg_ref[...])) * _dot(h, wu_ref[...])).astype(BF16)
            acc = _dot(a, wd_ref[...])
            if not first:
                acc = o_ref[rs, :] + acc
            if last:
                acc = x_ref[rs, :] + _rms(acc, 0.5 * gout_ref[...])
            o_ref[rs, :] = acc

    if nk == 1:
        step(True, True)
    else:
        pl.when(k == 0)(lambda: step(True, False))
        pl.when(jnp.logical_and(k > 0, k < nk - 1))(lambda: step(False, False))
        pl.when(k == nk - 1)(lambda: step(False, True))


def _ffn(x, gains, wg, wu, wd, layer, which, n_in, n_out):
    t, d = x.shape
    f = wg.shape[-1]
    tm = min(FFN_TOKENS, t)
    fc = min(FFN_CHUNK, f)
    rows = min(FFN_ROWS, tm)
    gain_spec = lambda n: pl.BlockSpec((None, None, 1, d), lambda i, k: (layer, n, 0, 0))
    vmem = (4 * tm * d * 4
            + 2 * 3 * d * fc * 2
            + tm * d * 2
            + 3 * rows * fc * 4
            + 2 * rows * d * 4)
    return pl.pallas_call(
        functools.partial(_ffn_kernel, rows=rows, nk=f // fc),
        grid=(t // tm, f // fc),
        in_specs=[
            pl.BlockSpec((tm, d), lambda i, k: (i, 0)),
            gain_spec(n_in),
            pl.BlockSpec((None, None, d, fc), lambda i, k: (layer, which, 0, k)),
            pl.BlockSpec((None, None, d, fc), lambda i, k: (layer, which, 0, k)),
            pl.BlockSpec((None, None, fc, d), lambda i, k: (layer, which, k, 0)),
            gain_spec(n_out),
        ],
        out_specs=pl.BlockSpec((tm, d), lambda i, k: (i, 0)),
        out_shape=jax.ShapeDtypeStruct((t, d), F32),
        scratch_shapes=[pltpu.VMEM((tm, d), BF16)],
        compiler_params=_params(("parallel", "arbitrary"), vmem),
        name="ffn",
    )(x, gains, wg, wu, wd, gains)


def _norm_matmul_kernel(x_ref, g_ref, w_ref, o_ref, h_ref, *, rows):
    def step(first):
        for r in range(0, x_ref.shape[0], rows):
            rs = slice(r, r + rows)
            if first:
                h = _rms(x_ref[rs, :], g_ref[...]).astype(BF16)
                h_ref[rs, :] = h
            else:
                h = h_ref[rs, :]
            o_ref[rs, :] = _dot(h, w_ref[...]).astype(o_ref.dtype)

    pl.when(pl.program_id(1) == 0)(lambda: step(True))
    pl.when(pl.program_id(1) > 0)(lambda: step(False))


def _norm_matmul(x, gains, w, layer, n_gain, j):
    t, d = x.shape
    n = w.shape[-1]
    tm = min(PROJ_TOKENS, t)
    tn = min(PROJ_COLS, n)
    vmem = 2 * tm * d * 4 + 2 * d * tn * 2 + 2 * tm * tn * 2 + tm * d * 2 + 2 * tm * d * 4 + tm * tn * 4
    return pl.pallas_call(
        functools.partial(_norm_matmul_kernel, rows=min(FFN_ROWS, tm)),
        grid=(t // tm, n // tn),
        in_specs=[
            pl.BlockSpec((tm, d), lambda i, k: (i, 0)),
            pl.BlockSpec((None, None, 1, d), lambda i, k: (layer, n_gain, 0, 0)),
            pl.BlockSpec((None, d, tn), lambda i, k: (j, 0, k)),
        ],
        out_specs=pl.BlockSpec((tm, tn), lambda i, k: (i, k)),
        out_shape=jax.ShapeDtypeStruct((t, n), BF16),
        scratch_shapes=[pltpu.VMEM((tm, d), BF16)],
        compiler_params=_params(("parallel", "arbitrary"), vmem),
        name="norm_in_proj",
    )(x, gains, w)


def _out_proj_kernel(ya_ref, yb_ref, x_ref, wa_ref, wb_ref, g_ref, o_ref, *, rows):
    for r in range(0, x_ref.shape[0], rows):
        rs = slice(r, r + rows)
        m = _dot(ya_ref[rs, :], wa_ref[...]) + _dot(yb_ref[rs, :], wb_ref[...])
        o_ref[rs, :] = x_ref[rs, :] + _rms(m, g_ref[...])


def _out_proj(ya, yb, x, gains, w, layer, n_gain, j):
    t, d = x.shape
    half = ya.shape[-1]
    tm = min(ROW_TOKENS, t)
    rows = min(ROW_GROUP, tm)
    resident = pl.Buffered(1)
    vmem = 2 * 2 * tm * half * 2 + 4 * tm * d * 4 + 2 * half * d * 2 + 3 * rows * d * 4
    return pl.pallas_call(
        functools.partial(_out_proj_kernel, rows=rows),
        grid=(t // tm,),
        in_specs=[
            pl.BlockSpec((tm, half), lambda i: (i, 0)),
            pl.BlockSpec((tm, half), lambda i: (i, 0)),
            pl.BlockSpec((tm, d), lambda i: (i, 0)),
            pl.BlockSpec((None, half, d), lambda i: (j, 0, 0), pipeline_mode=resident),
            pl.BlockSpec((None, half, d), lambda i: (j, 1, 0), pipeline_mode=resident),
            pl.BlockSpec((None, None, 1, d), lambda i: (layer, n_gain, 0, 0)),
        ],
        out_specs=pl.BlockSpec((tm, d), lambda i: (i, 0)),
        out_shape=jax.ShapeDtypeStruct((t, d), F32),
        compiler_params=_params(("parallel",), vmem),
        name="out_proj",
    )(ya, yb, x, w, w, gains)


def _ple_kernel(x_ref, p_ref, gin_ref, wgate_ref, wproj_ref, gout_ref, o_ref, *, rows):
    for r in range(0, x_ref.shape[0], rows):
        rs = slice(r, r + rows)
        x = x_ref[rs, :]
        gate = jax.nn.sigmoid(_dot(_rms(x, gin_ref[...]).astype(BF16), wgate_ref[...]))
        emb = _dot(p_ref[rs, :].astype(BF16), wproj_ref[...])
        o_ref[rs, :] = x + _rms(gate * emb, gout_ref[...])


def _ple(x, p, gains, wgate, wproj, layer):
    t, d = x.shape
    e = p.shape[-1]
    tm = min(ROW_TOKENS, t)
    rows = min(ROW_GROUP, tm)
    resident = pl.Buffered(1)
    vmem = 4 * tm * d * 4 + 2 * tm * e * 4 + d * d * 2 + e * d * 2 + 5 * rows * d * 4
    return pl.pallas_call(
        functools.partial(_ple_kernel, rows=rows),
        grid=(t // tm,),
        in_specs=[
            pl.BlockSpec((tm, d), lambda i: (i, 0)),
            pl.BlockSpec((None, tm, e), lambda i: (layer, i, 0)),
            pl.BlockSpec((None, None, 1, d), lambda i: (layer, 6, 0, 0)),
            pl.BlockSpec((None, d, d), lambda i: (layer, 0, 0), pipeline_mode=resident),
            pl.BlockSpec((None, e, d), lambda i: (layer, 0, 0), pipeline_mode=resident),
            pl.BlockSpec((None, None, 1, d), lambda i: (layer, 7, 0, 0)),
        ],
        out_specs=pl.BlockSpec((tm, d), lambda i: (i, 0)),
        out_shape=jax.ShapeDtypeStruct((t, d), F32),
        compiler_params=_params(("parallel",), vmem),
        name="ple",
    )(x, p, gains, wgate, wproj, gains)


def _retention_tables(seq, hd):
    half = hd // 2
    inv = 1.0 / np.power(np.float32(RET_THETA), np.arange(half, dtype=np.float32) / half)
    ang = np.arange(seq, dtype=np.float32)[:, None] * inv[None, :]
    c = RET_CHUNK
    log_gamma = np.log(1.0 - np.power(2.0, -5.0 - np.arange(RET_HEADS, dtype=np.float64)))
    pos = np.arange(c, dtype=np.float64)
    rel = pos[:, None] - pos[None, :]
    intra = np.exp(np.maximum(rel, 0.0)[None] * log_gamma[:, None, None]) * (rel >= 0)[None]
    qd = np.broadcast_to(np.exp((pos + 1.0)[None, :] * log_gamma[:, None])[..., None], (RET_HEADS, c, hd))
    kd = np.broadcast_to(np.exp((c - 1.0 - pos)[None, :] * log_gamma[:, None])[..., None], (RET_HEADS, c, hd))
    cd = np.broadcast_to(np.exp(c * log_gamma)[:, None, None], (RET_HEADS, 1, hd))
    as32 = lambda a: jnp.asarray(np.ascontiguousarray(a), F32)
    return as32(np.cos(ang)), as32(np.sin(ang)), as32(intra), as32(qd), as32(kd), as32(cd)


def _even_core_kernel(q_ref, k_ref, v_ref, g_ref, u_ref, vs_ref, cos_ref, sin_ref, dec_ref, qd_ref, kd_ref,
                      cd_ref, ws_ref, bs_ref, gain_ref, oret_ref, osgu_ref):
    seq, hd = q_ref.shape
    half = hd // 2
    c = ws_ref.shape[-1]
    dec = dec_ref[...]
    qd = qd_ref[...]
    kd = kd_ref[...]
    cd = cd_ref[...]
    row = lax.broadcasted_iota(jnp.int32, (c, c), 0)
    col = lax.broadcasted_iota(jnp.int32, (c, c), 1)
    w = jnp.where(row >= col, ws_ref[...], 0.0).astype(BF16)
    bias = bs_ref[...]
    gain = gain_ref[...]

    state = jnp.zeros((hd, hd), F32)
    for i in range(seq // c):
        rs = slice(i * c, (i + 1) * c)
        cos = cos_ref[rs, :]
        sin = sin_ref[rs, :]

        def rotary(t):
            t1 = t[:, :half]
            t2 = t[:, half:]
            return jnp.concatenate([t1 * cos - t2 * sin, t2 * cos + t1 * sin], axis=-1)

        q = rotary(q_ref[rs, :].astype(F32))
        k = rotary(k_ref[rs, :].astype(F32)) * (hd ** -0.5)
        v = v_ref[rs, :]
        scores = _dot_nt(q.astype(BF16), k.astype(BF16)) * dec
        out = _dot(scores.astype(BF16), v) + _dot((q * qd).astype(BF16), state.astype(BF16))
        state = state * cd + _dot_tn((k * kd).astype(BF16), v)
        oret_ref[rs, :] = (_rms(out) * jax.nn.silu(g_ref[rs, :].astype(F32))).astype(oret_ref.dtype)

        vn = _rms(jax.nn.gelu(vs_ref[rs, :].astype(F32)), gain).astype(BF16)
        sv = _dot(w, vn) + bias
        osgu_ref[rs, :] = (jax.nn.gelu(u_ref[rs, :].astype(F32)) * sv).astype(osgu_ref.dtype)


def _even_core(z, batch, seq, sgu_w, sgu_b, sgu_gain, j):
    t = z.shape[0]
    mix = z.shape[1] // 6
    hd = mix // RET_HEADS
    c = RET_CHUNK
    cos, sin, intra, qd, kd, cd = _retention_tables(seq, hd)
    zspec = lambda part: pl.BlockSpec((seq, hd), lambda b, h: (b, part * RET_HEADS + h))
    table = pl.BlockSpec((seq, hd // 2), lambda b, h: (0, 0))
    head3 = lambda r, w: pl.BlockSpec((None, r, w), lambda b, h: (h, 0, 0))
    param = lambda r, w: pl.BlockSpec((None, None, r, w), lambda b, h: (j, h, 0, 0))
    out_spec = pl.BlockSpec((seq, hd), lambda b, h: (b, h))
    vmem = 2 * (6 * seq * hd * 4 + 2 * seq * (hd // 2) * 4 + 2 * seq * hd * 2) + 32 * c * hd * 4 + 16 * hd * hd * 4
    return pl.pallas_call(
        _even_core_kernel,
        grid=(batch, RET_HEADS),
        in_specs=[zspec(0), zspec(1), zspec(2), zspec(3), zspec(4), zspec(5), table, table,
                  head3(c, c), head3(c, hd), head3(c, hd), head3(1, hd),
                  param(c, c), param(c, 1), param(1, hd)],
        out_specs=[out_spec, out_spec],
        out_shape=[jax.ShapeDtypeStruct((t, mix), BF16), jax.ShapeDtypeStruct((t, mix), BF16)],
        compiler_params=_params(("parallel", "parallel"), vmem),
        name="even_core",
    )(z, z, z, z, z, z, cos, sin, intra, qd, kd, cd, sgu_w, sgu_b, sgu_gain)


def _pool_kernel(z_ref, w_ref, scale_ref, o_ref):
    z = z_ref[...].astype(F32)
    gi = pl.program_id(1)
    t = lax.broadcasted_iota(jnp.int32, z.shape, 0)

    def shifted(a, sh):
        return jnp.where(t >= sh, pltpu.roll(a, sh, 0), 0.0)

    sums = []
    s = z
    sh = 1
    for _ in POOL_WINDOWS:
        s = s + shifted(s, sh)
        sums.append(s)
        sh *= 2
    win = sums[-1]
    for idx in range(len(POOL_WINDOWS) - 2, -1, -1):
        win = jnp.where(gi == idx, sums[idx], win)
    width = jnp.left_shift(jnp.int32(POOL_WINDOWS[0]), gi)
    cnt = jnp.minimum(t + 1, width).astype(F32)
    y = win / cnt - z
    o_ref[...] = (_dot(y.astype(BF16), w_ref[...]) * scale_ref[...]).astype(o_ref.dtype)


def _pool(z, batch, seq, pool_w, pool_scale, j):
    assert all(w == POOL_WINDOWS[0] << i for i, w in enumerate(POOL_WINDOWS))
    t = z.shape[0]
    mix = z.shape[1] // 4
    ng = len(POOL_WINDOWS)
    dg = mix // ng
    vmem = 2 * (seq * dg * 4 + dg * dg * 2 + seq * dg * 2) + 10 * seq * dg * 4
    return pl.pallas_call(
        _pool_kernel,
        grid=(batch, ng),
        in_specs=[pl.BlockSpec((seq, dg), lambda b, g: (b, g)),
                  pl.BlockSpec((None, None, dg, dg), lambda b, g: (j, g, 0, 0)),
                  pl.BlockSpec((None, None, 1, dg), lambda b, g: (j, g, 0, 0))],
        out_specs=pl.BlockSpec((seq, dg), lambda b, g: (b, g)),
        out_shape=jax.ShapeDtypeStruct((t, mix), BF16),
        compiler_params=_params(("parallel", "parallel"), vmem),
        name="pool",
    )(z, pool_w, pool_scale)


def _moba_tables(seq, hd):
    rot = hd // 4
    half = rot // 2
    inv = 1.0 / np.power(np.float32(ROPE_THETA), np.arange(half, dtype=np.float32) / half)
    ang = np.arange(seq, dtype=np.float32)[:, None] * inv[None, :]
    cos = np.concatenate([np.cos(ang), np.cos(ang), np.ones((seq, hd - rot), np.float32)], axis=1)
    sin = np.concatenate([-np.sin(ang), np.sin(ang), np.zeros((seq, hd - rot), np.float32)], axis=1)
    return jnp.asarray(cos, F32), jnp.asarray(sin, F32)


def _moba_kernel(q_ref, k_ref, v_ref, cos_ref, sin_ref, o_ref, *, hd):
    views = [[r.at[:, pl.ds(head * hd, hd)] for r in (q_ref, k_ref, v_ref, o_ref)]
             for head in range(q_ref.shape[1] // hd)]
    operands = [{} for _ in views]
    prepare = [_moba_prepare(v[0], v[1], v[2], cos_ref, sin_ref, out) for v, out in zip(views, operands)]
    for _ in prepare[0]:
        pass
    for head, view in enumerate(views):
        upcoming = prepare[head + 1] if head + 1 < len(views) else iter(())
        for _ in _moba_attend(operands[head], view[3]):
            next(upcoming, None)
        for _ in upcoming:
            pass


def _moba_prepare(q_ref, k_ref, v_ref, cos_ref, sin_ref, out):
    seq, hd = q_ref.shape
    bs = MOBA_BLOCK
    nb = seq // bs
    n_sel = min(MOBA_TOPK, nb)
    half = hd // 8
    lane = lax.broadcasted_iota(jnp.int32, (seq, hd), 1)

    def rotary(t):
        partner = jnp.where(lane < half, pltpu.roll(t, hd - half, 1), pltpu.roll(t, half, 1))
        return t * cos_ref[...] + partner * sin_ref[...]

    q = rotary(q_ref[...].astype(F32))
    yield
    k = rotary(k_ref[...].astype(F32))
    k_mean = jnp.mean(k.reshape(nb, bs, hd), axis=1)
    yield
    qt = q.T
    gate = jnp.dot(k_mean, qt, precision=lax.Precision.HIGHEST, preferred_element_type=F32)
    yield

    q_blk = lax.broadcasted_iota(jnp.int32, (1, seq), 1) // bs
    bias_rows = []
    for n in range(nb):
        mine = gate[n:n + 1]
        rank = jnp.zeros((1, seq), F32)
        for o in range(nb):
            if o != n:
                other = gate[o:o + 1]
                ahead = (other >= mine) if o < n else (other > mine)
                rank = rank + jnp.where(jnp.logical_and(ahead, q_blk > o), 1.0, 0.0)
        keep = jnp.logical_or(rank < n_sel, q_blk <= n)
        bias_rows.append(jnp.where(keep, 0.0, NEG))
    yield

    scale = hd ** -0.5 * LOG2_E
    out["q"] = jnp.concatenate([qt * scale] + bias_rows + [jnp.zeros((hd - nb, seq), F32)], axis=0).astype(BF16)
    yield
    k_blk = lax.broadcasted_iota(jnp.int32, (seq, hd), 0) // bs
    one_hot = jnp.where(k_blk == lax.broadcasted_iota(jnp.int32, (seq, hd), 1), 1.0, 0.0)
    out["k"] = jnp.concatenate([k, one_hot], axis=1).astype(BF16)
    yield
    vt = v_ref[...].astype(F32).T
    ones_rows = jnp.where(lax.broadcasted_iota(jnp.int32, (DENOM_ROWS, seq), 0) == 0, 1.0, 0.0)
    out["v"] = jnp.concatenate([vt, ones_rows], axis=0).astype(BF16)


def _moba_attend(operands, o_ref):
    q_aug, k_aug, v_aug = operands["q"], operands["k"], operands["v"]
    seq, hd = o_ref.shape
    bs = MOBA_BLOCK
    key = lax.broadcasted_iota(jnp.int32, (bs, bs), 0)
    qry = lax.broadcasted_iota(jnp.int32, (bs, bs), 1)
    for qb in range(seq // bs):
        qs = slice(qb * bs, (qb + 1) * bs)
        qq = q_aug[:, qs]
        logits = []
        for jb in range(qb + 1):
            lg = _dot(k_aug[jb * bs:(jb + 1) * bs], qq)
            if jb == qb:
                lg = jnp.where(key <= qry, lg, NEG)
            logits.append(lg)
        m = functools.reduce(jnp.maximum, [jnp.max(lg, axis=0, keepdims=True) for lg in logits])
        acc = functools.reduce(jnp.add, [_dot(v_aug[:, jb * bs:(jb + 1) * bs], jnp.exp2(lg - m).astype(BF16))
                                         for jb, lg in enumerate(logits)])
        o_ref[qs, :] = (acc[:hd] / acc[hd:hd + 1]).T.astype(o_ref.dtype)
        yield


def _moba(z, batch, seq):
    assert seq % MOBA_BLOCK == 0
    t = z.shape[0]
    mix = z.shape[1] // 4
    hd = mix // MOBA_HEADS
    cos, sin = _moba_tables(seq, hd)
    hps = MOBA_HEADS_PER_STEP
    groups = MOBA_HEADS // hps
    zspec = lambda part: pl.BlockSpec((seq, hps * hd), lambda b, g: (b, part * groups + g))
    table = pl.BlockSpec((seq, hd), lambda b, g: (0, 0))
    nb = seq // MOBA_BLOCK
    vmem = 2 * (3 * seq * hps * hd * 2 + 2 * seq * hd * 4 + seq * hps * hd * 2) \
        + hps * (10 * seq * hd * 4 + 3 * nb * MOBA_BLOCK * MOBA_BLOCK * 4)
    return pl.pallas_call(
        functools.partial(_moba_kernel, hd=hd),
        grid=(batch, groups),
        in_specs=[zspec(1), zspec(2), zspec(3), table, table],
        out_specs=pl.BlockSpec((seq, hps * hd), lambda b, g: (b, g)),
        out_shape=jax.ShapeDtypeStruct((t, mix), BF16),
        compiler_params=_params(("parallel", "parallel"), vmem),
        name="moba",
    )(z, z, z, cos, sin)


def kernel(x, p, norm_gains, w_ffn_gate, w_ffn_up, w_ffn_down, w_in_even, w_out_even, sgu_w, sgu_b, sgu_gain,
           w_in_odd, w_out_odd, pool_w, pool_scale, w_ple_gate, w_ple_proj):
    batch, seq, d = x.shape
    depth = norm_gains.shape[0]
    t = batch * seq
    assert seq % RET_CHUNK == 0 and RET_CHUNK == SGU_CHUNK and RET_HEADS == SGU_GROUPS

    gains = norm_gains.reshape(depth, norm_gains.shape[1], 1, d)
    wg, wu, wd = (w.astype(BF16) for w in (w_ffn_gate, w_ffn_up, w_ffn_down))
    w_in_e, w_out_e, w_in_o, w_out_o = (w.astype(BF16) for w in (w_in_even, w_out_even, w_in_odd, w_out_odd))
    wpg, wpp = w_ple_gate.astype(BF16), w_ple_proj.astype(BF16)
    pool_w16 = pool_w.astype(BF16)
    n_even = sgu_w.shape[0]
    sgu_b4 = sgu_b.reshape(n_even, SGU_GROUPS, SGU_CHUNK, 1)
    sgu_gain4 = sgu_gain.reshape(n_even, SGU_GROUPS, 1, -1)
    pool_scale4 = pool_scale.reshape(pool_scale.shape[0], len(POOL_WINDOWS), 1, -1)
    p3 = p.reshape(depth, t, p.shape[-1])

    xf = x.reshape(t, d)
    for i in range(depth):
        j = i // 2
        xf = _ffn(xf, gains, wg, wu, wd, i, 0, 0, 1)
        if i % 2 == 0:
            z = _norm_matmul(xf, gains, w_in_e, i, 2, j)
            ya, yb = _even_core(z, batch, seq, sgu_w, sgu_b4, sgu_gain4, j)
            xf = _out_proj(ya, yb, xf, gains, w_out_e, i, 3, j)
        else:
            z = _norm_matmul(xf, gains, w_in_o, i, 2, j)
            ya = _pool(z, batch, seq, pool_w16, pool_scale4, j)
            yb = _moba(z, batch, seq)
            xf = _out_proj(ya, yb, xf, gains, w_out_o, i, 3, j)
        xf = _ffn(xf, gains, wg, wu, wd, i, 1, 4, 5)
        xf = _ple(xf, p3, gains, wpg, wpp, i)
    return xf.reshape(batch, seq, d)
```

```python
import functools

import numpy as np
import jax
import jax.numpy as jnp
from jax import lax
from jax.experimental import pallas as pl
from jax.experimental.pallas import tpu as pltpu

F32 = jnp.float32
BF16 = jnp.bfloat16

RET_HEADS = 4
RET_CHUNK = 128
RET_THETA = 10000.0
SGU_GROUPS = 4
SGU_CHUNK = 128
POOL_WINDOWS = (2, 4, 8, 16)
MOBA_HEADS = 8
MOBA_BLOCK = 256
MOBA_TOPK = 3
MOBA_HEADS_PER_STEP = 4
ROPE_THETA = 500000.0
EPS = 1e-6
NEG = -1e30
LOG2_E = 1.4426950408889634
DENOM_ROWS = 16

V7X_VMEM_BYTES = 64 * 1024 * 1024
VMEM_REQUEST_CAP = V7X_VMEM_BYTES - 8 * 1024 * 1024

FFN_TOKENS = 1024
FFN_CHUNK = 512
FFN_ROWS = 512
PROJ_TOKENS = 1024
PROJ_COLS = 2048
ROW_TOKENS = 512


def _params(semantics, vmem_bytes):
    return pltpu.CompilerParams(dimension_semantics=semantics,
                                vmem_limit_bytes=int(min(vmem_bytes, VMEM_REQUEST_CAP)))


def _rms(x, gain=None):
    y = x * lax.rsqrt(jnp.mean(x * x, axis=-1, keepdims=True) + EPS)
    return y if gain is None else y * gain


def _dot(a, b):
    return jnp.dot(a, b, preferred_element_type=F32)


def _dot_nt(a, b, precision=None):
    return lax.dot_general(a, b, (((1,), (1,)), ((), ())), precision=precision, preferred_element_type=F32)


def _dot_tn(a, b):
    return lax.dot_general(a, b, (((0,), (0,)), ((), ())), preferred_element_type=F32)


def _ffn_kernel(x_ref, gin_ref, wg_ref, wu_ref, wd_ref, gout_ref, o_ref, h_ref, *, rows, nk):
    k = pl.program_id(1)
    tm = x_ref.shape[0]

    def step(first, last):
        for r in range(0, tm, rows):
            rs = slice(r, r + rows)
            if first:
                h = _rms(x_ref[rs, :], gin_ref[...]).astype(BF16)
                h_ref[rs, :] = h
            else:
                h = h_ref[rs, :]
            a = (jax.nn.silu(_dot(h, wg_ref[...])) * _dot(h, wu_ref[...])).astype(BF16)
            acc = _dot(a, wd_ref[...])
            if not first:
                acc = o_ref[rs, :] + acc
            if last:
                acc = x_ref[rs, :] + _rms(acc, 0.5 * gout_ref[...])
            o_ref[rs, :] = acc

    if nk == 1:
        step(True, True)
    else:
        pl.when(k == 0)(lambda: step(True, False))
        pl.when(jnp.logical_and(k > 0, k < nk - 1))(lambda: step(False, False))
        pl.when(k == nk - 1)(lambda: step(False, True))


def _ffn(x, gains, wg, wu, wd, layer, which, n_in, n_out):
    t, d = x.shape
    f = wg.shape[-1]
    tm = min(FFN_TOKENS, t)
    fc = min(FFN_CHUNK, f)
    rows = min(FFN_ROWS, tm)
    gain_spec = lambda n: pl.BlockSpec((None, None, 1, d), lambda i, k: (layer, n, 0, 0))
    vmem = (4 * tm * d * 4
            + 2 * 3 * d * fc * 2
            + tm * d * 2
            + 3 * rows * fc * 4
            + 2 * rows * d * 4)
    return pl.pallas_call(
        functools.partial(_ffn_kernel, rows=rows, nk=f // fc),
        grid=(t // tm, f // fc),
        in_specs=[
            pl.BlockSpec((tm, d), lambda i, k: (i, 0)),
            gain_spec(n_in),
            pl.BlockSpec((None, None, d, fc), lambda i, k: (layer, which, 0, k)),
            pl.BlockSpec((None, None, d, fc), lambda i, k: (layer, which, 0, k)),
            pl.BlockSpec((None, None, fc, d), lambda i, k: (layer, which, k, 0)),
            gain_spec(n_out),
        ],
        out_specs=pl.BlockSpec((tm, d), lambda i, k: (i, 0)),
        out_shape=jax.ShapeDtypeStruct((t, d), F32),
        scratch_shapes=[pltpu.VMEM((tm, d), BF16)],
        compiler_params=_params(("parallel", "arbitrary"), vmem),
        name="ffn",
    )(x, gains, wg, wu, wd, gains)


def _norm_matmul_kernel(x_ref, g_ref, w_ref, o_ref, h_ref, *, rows):
    def step(first):
        for r in range(0, x_ref.shape[0], rows):
            rs = slice(r, r + rows)
            if first:
                h = _rms(x_ref[rs, :], g_ref[...]).astype(BF16)
                h_ref[rs, :] = h
            else:
                h = h_ref[rs, :]
            o_ref[rs, :] = _dot(h, w_ref[...]).astype(o_ref.dtype)

    pl.when(pl.program_id(1) == 0)(lambda: step(True))
    pl.when(pl.program_id(1) > 0)(lambda: step(False))


def _norm_matmul(x, gains, w, layer, n_gain, j):
    t, d = x.shape
    n = w.shape[-1]
    tm = min(PROJ_TOKENS, t)
    tn = min(PROJ_COLS, n)
    vmem = 2 * tm * d * 4 + 2 * d * tn * 2 + 2 * tm * tn * 2 + tm * d * 2 + 2 * tm * d * 4 + tm * tn * 4
    return pl.pallas_call(
        functools.partial(_norm_matmul_kernel, rows=min(FFN_ROWS, tm)),
        grid=(t // tm, n // tn),
        in_specs=[
            pl.BlockSpec((tm, d), lambda i, k: (i, 0)),
            pl.BlockSpec((None, None, 1, d), lambda i, k: (layer, n_gain, 0, 0)),
            pl.BlockSpec((None, d, tn), lambda i, k: (j, 0, k)),
        ],
        out_specs=pl.BlockSpec((tm, tn), lambda i, k: (i, k)),
        out_shape=jax.ShapeDtypeStruct((t, n), BF16),
        scratch_shapes=[pltpu.VMEM((tm, d), BF16)],
        compiler_params=_params(("parallel", "arbitrary"), vmem),
        name="norm_in_proj",
    )(x, gains, w)


def _out_proj_kernel(ya_ref, yb_ref, x_ref, wa_ref, wb_ref, g_ref, o_ref):
    m = _dot(ya_ref[...], wa_ref[...]) + _dot(yb_ref[...], wb_ref[...])
    o_ref[...] = x_ref[...] + _rms(m, g_ref[...])


def _out_proj(ya, yb, x, gains, w, layer, n_gain, j):
    t, d = x.shape
    half = ya.shape[-1]
    tm = min(ROW_TOKENS, t)
    vmem = 2 * 2 * tm * half * 2 + 4 * tm * d * 4 + 2 * 2 * half * d * 2 + 3 * tm * d * 4
    return pl.pallas_call(
        _out_proj_kernel,
        grid=(t // tm,),
        in_specs=[
            pl.BlockSpec((tm, half), lambda i: (i, 0)),
            pl.BlockSpec((tm, half), lambda i: (i, 0)),
            pl.BlockSpec((tm, d), lambda i: (i, 0)),
            pl.BlockSpec((None, half, d), lambda i: (j, 0, 0)),
            pl.BlockSpec((None, half, d), lambda i: (j, 1, 0)),
            pl.BlockSpec((None, None, 1, d), lambda i: (layer, n_gain, 0, 0)),
        ],
        out_specs=pl.BlockSpec((tm, d), lambda i: (i, 0)),
        out_shape=jax.ShapeDtypeStruct((t, d), F32),
        compiler_params=_params(("parallel",), vmem),
        name="out_proj",
    )(ya, yb, x, w, w, gains)


def _ple_kernel(x_ref, p_ref, gin_ref, wgate_ref, wproj_ref, gout_ref, o_ref):
    x = x_ref[...]
    gate = jax.nn.sigmoid(_dot(_rms(x, gin_ref[...]).astype(BF16), wgate_ref[...]))
    emb = _dot(p_ref[...].astype(BF16), wproj_ref[...])
    o_ref[...] = x + _rms(gate * emb, gout_ref[...])


def _ple(x, p, gains, wgate, wproj, layer):
    t, d = x.shape
    e = p.shape[-1]
    tm = min(ROW_TOKENS, t)
    vmem = 4 * tm * d * 4 + 2 * tm * e * 4 + 2 * d * d * 2 + 2 * e * d * 2 + 4 * tm * d * 4
    return pl.pallas_call(
        _ple_kernel,
        grid=(t // tm,),
        in_specs=[
            pl.BlockSpec((tm, d), lambda i: (i, 0)),
            pl.BlockSpec((None, tm, e), lambda i: (layer, i, 0)),
            pl.BlockSpec((None, None, 1, d), lambda i: (layer, 6, 0, 0)),
            pl.BlockSpec((None, d, d), lambda i: (layer, 0, 0)),
            pl.BlockSpec((None, e, d), lambda i: (layer, 0, 0)),
            pl.BlockSpec((None, None, 1, d), lambda i: (layer, 7, 0, 0)),
        ],
        out_specs=pl.BlockSpec((tm, d), lambda i: (i, 0)),
        out_shape=jax.ShapeDtypeStruct((t, d), F32),
        compiler_params=_params(("parallel",), vmem),
        name="ple",
    )(x, p, gains, wgate, wproj, gains)


def _retention_tables(seq, hd):
    half = hd // 2
    inv = 1.0 / np.power(np.float32(RET_THETA), np.arange(half, dtype=np.float32) / half)
    ang = np.arange(seq, dtype=np.float32)[:, None] * inv[None, :]
    c = RET_CHUNK
    log_gamma = np.log(1.0 - np.power(2.0, -5.0 - np.arange(RET_HEADS, dtype=np.float64)))
    pos = np.arange(c, dtype=np.float64)
    rel = pos[:, None] - pos[None, :]
    intra = np.exp(np.maximum(rel, 0.0)[None] * log_gamma[:, None, None]) * (rel >= 0)[None]
    qd = np.broadcast_to(np.exp((pos + 1.0)[None, :] * log_gamma[:, None])[..., None], (RET_HEADS, c, hd))
    kd = np.broadcast_to(np.exp((c - 1.0 - pos)[None, :] * log_gamma[:, None])[..., None], (RET_HEADS, c, hd))
    cd = np.broadcast_to(np.exp(c * log_gamma)[:, None, None], (RET_HEADS, 1, hd))
    as32 = lambda a: jnp.asarray(np.ascontiguousarray(a), F32)
    return as32(np.cos(ang)), as32(np.sin(ang)), as32(intra), as32(qd), as32(kd), as32(cd)


def _even_core_kernel(q_ref, k_ref, v_ref, g_ref, u_ref, vs_ref, cos_ref, sin_ref, dec_ref, qd_ref, kd_ref,
                      cd_ref, ws_ref, bs_ref, gain_ref, oret_ref, osgu_ref):
    seq, hd = q_ref.shape
    half = hd // 2
    c = ws_ref.shape[-1]
    dec = dec_ref[...]
    qd = qd_ref[...]
    kd = kd_ref[...]
    cd = cd_ref[...]
    row = lax.broadcasted_iota(jnp.int32, (c, c), 0)
    col = lax.broadcasted_iota(jnp.int32, (c, c), 1)
    w = jnp.where(row >= col, ws_ref[...], 0.0).astype(BF16)
    bias = bs_ref[...]
    gain = gain_ref[...]

    state = jnp.zeros((hd, hd), F32)
    for i in range(seq // c):
        rs = slice(i * c, (i + 1) * c)
        cos = cos_ref[rs, :]
        sin = sin_ref[rs, :]

        def rotary(t):
            t1 = t[:, :half]
            t2 = t[:, half:]
            return jnp.concatenate([t1 * cos - t2 * sin, t2 * cos + t1 * sin], axis=-1)

        q = rotary(q_ref[rs, :].astype(F32))
        k = rotary(k_ref[rs, :].astype(F32)) * (hd ** -0.5)
        v = v_ref[rs, :]
        scores = _dot_nt(q.astype(BF16), k.astype(BF16)) * dec
        out = _dot(scores.astype(BF16), v) + _dot((q * qd).astype(BF16), state.astype(BF16))
        state = state * cd + _dot_tn((k * kd).astype(BF16), v)
        oret_ref[rs, :] = (_rms(out) * jax.nn.silu(g_ref[rs, :].astype(F32))).astype(oret_ref.dtype)

        vn = _rms(jax.nn.gelu(vs_ref[rs, :].astype(F32)), gain).astype(BF16)
        sv = _dot(w, vn) + bias
        osgu_ref[rs, :] = (jax.nn.gelu(u_ref[rs, :].astype(F32)) * sv).astype(osgu_ref.dtype)


def _even_core(z, batch, seq, sgu_w, sgu_b, sgu_gain, j):
    t = z.shape[0]
    mix = z.shape[1] // 6
    hd = mix // RET_HEADS
    c = RET_CHUNK
    cos, sin, intra, qd, kd, cd = _retention_tables(seq, hd)
    zspec = lambda part: pl.BlockSpec((seq, hd), lambda b, h: (b, part * RET_HEADS + h))
    table = pl.BlockSpec((seq, hd // 2), lambda b, h: (0, 0))
    head3 = lambda r, w: pl.BlockSpec((None, r, w), lambda b, h: (h, 0, 0))
    param = lambda r, w: pl.BlockSpec((None, None, r, w), lambda b, h: (j, h, 0, 0))
    out_spec = pl.BlockSpec((seq, hd), lambda b, h: (b, h))
    vmem = 2 * (6 * seq * hd * 4 + 2 * seq * (hd // 2) * 4 + 2 * seq * hd * 2) + 32 * c * hd * 4 + 16 * hd * hd * 4
    return pl.pallas_call(
        _even_core_kernel,
        grid=(batch, RET_HEADS),
        in_specs=[zspec(0), zspec(1), zspec(2), zspec(3), zspec(4), zspec(5), table, table,
                  head3(c, c), head3(c, hd), head3(c, hd), head3(1, hd),
                  param(c, c), param(c, 1), param(1, hd)],
        out_specs=[out_spec, out_spec],
        out_shape=[jax.ShapeDtypeStruct((t, mix), BF16), jax.ShapeDtypeStruct((t, mix), BF16)],
        compiler_params=_params(("parallel", "parallel"), vmem),
        name="even_core",
    )(z, z, z, z, z, z, cos, sin, intra, qd, kd, cd, sgu_w, sgu_b, sgu_gain)


def _pool_kernel(z_ref, w_ref, scale_ref, o_ref):
    z = z_ref[...].astype(F32)
    gi = pl.program_id(1)
    t = lax.broadcasted_iota(jnp.int32, z.shape, 0)

    def shifted(a, sh):
        return jnp.where(t >= sh, pltpu.roll(a, sh, 0), 0.0)

    sums = []
    s = z
    sh = 1
    for _ in POOL_WINDOWS:
        s = s + shifted(s, sh)
        sums.append(s)
        sh *= 2
    win = sums[-1]
    for idx in range(len(POOL_WINDOWS) - 2, -1, -1):
        win = jnp.where(gi == idx, sums[idx], win)
    width = jnp.left_shift(jnp.int32(POOL_WINDOWS[0]), gi)
    cnt = jnp.minimum(t + 1, width).astype(F32)
    y = win / cnt - z
    o_ref[...] = (_dot(y.astype(BF16), w_ref[...]) * scale_ref[...]).astype(o_ref.dtype)


def _pool(z, batch, seq, pool_w, pool_scale, j):
    assert all(w == POOL_WINDOWS[0] << i for i, w in enumerate(POOL_WINDOWS))
    t = z.shape[0]
    mix = z.shape[1] // 4
    ng = len(POOL_WINDOWS)
    dg = mix // ng
    vmem = 2 * (seq * dg * 4 + dg * dg * 2 + seq * dg * 2) + 10 * seq * dg * 4
    return pl.pallas_call(
        _pool_kernel,
        grid=(batch, ng),
        in_specs=[pl.BlockSpec((seq, dg), lambda b, g: (b, g)),
                  pl.BlockSpec((None, None, dg, dg), lambda b, g: (j, g, 0, 0)),
                  pl.BlockSpec((None, None, 1, dg), lambda b, g: (j, g, 0, 0))],
        out_specs=pl.BlockSpec((seq, dg), lambda b, g: (b, g)),
        out_shape=jax.ShapeDtypeStruct((t, mix), BF16),
        compiler_params=_params(("parallel", "parallel"), vmem),
        name="pool",
    )(z, pool_w, pool_scale)


def _moba_tables(seq, hd):
    rot = hd // 4
    half = rot // 2
    inv = 1.0 / np.power(np.float32(ROPE_THETA), np.arange(half, dtype=np.float32) / half)
    ang = np.arange(seq, dtype=np.float32)[:, None] * inv[None, :]
    cos = np.concatenate([np.cos(ang), np.cos(ang), np.ones((seq, hd - rot), np.float32)], axis=1)
    sin = np.concatenate([-np.sin(ang), np.sin(ang), np.zeros((seq, hd - rot), np.float32)], axis=1)
    return jnp.asarray(cos, F32), jnp.asarray(sin, F32)


def _moba_kernel(q_ref, k_ref, v_ref, cos_ref, sin_ref, o_ref, *, hd):
    views = [[r.at[:, pl.ds(head * hd, hd)] for r in (q_ref, k_ref, v_ref, o_ref)]
             for head in range(q_ref.shape[1] // hd)]
    operands = [{} for _ in views]
    prepare = [_moba_prepare(v[0], v[1], v[2], cos_ref, sin_ref, out) for v, out in zip(views, operands)]
    for _ in prepare[0]:
        pass
    for head, view in enumerate(views):
        upcoming = prepare[head + 1] if head + 1 < len(views) else iter(())
        for _ in _moba_attend(operands[head], view[3]):
            next(upcoming, None)
        for _ in upcoming:
            pass


def _moba_prepare(q_ref, k_ref, v_ref, cos_ref, sin_ref, out):
    seq, hd = q_ref.shape
    bs = MOBA_BLOCK
    nb = seq // bs
    n_sel = min(MOBA_TOPK, nb)
    half = hd // 8
    lane = lax.broadcasted_iota(jnp.int32, (seq, hd), 1)

    def rotary(t):
        partner = jnp.where(lane < half, pltpu.roll(t, hd - half, 1), pltpu.roll(t, half, 1))
        return t * cos_ref[...] + partner * sin_ref[...]

    q = rotary(q_ref[...].astype(F32))
    yield
    k = rotary(k_ref[...].astype(F32))
    k_mean = jnp.mean(k.reshape(nb, bs, hd), axis=1)
    yield
    qt = q.T
    gate = jnp.dot(k_mean, qt, precision=lax.Precision.HIGHEST, preferred_element_type=F32)
    yield

    q_blk = lax.broadcasted_iota(jnp.int32, (1, seq), 1) // bs
    bias_rows = []
    for n in range(nb):
        mine = gate[n:n + 1]
        rank = jnp.zeros((1, seq), F32)
        for o in range(nb):
            if o != n:
                other = gate[o:o + 1]
                ahead = (other >= mine) if o < n else (other > mine)
                rank = rank + jnp.where(jnp.logical_and(ahead, q_blk > o), 1.0, 0.0)
        keep = jnp.logical_or(rank < n_sel, q_blk <= n)
        bias_rows.append(jnp.where(keep, 0.0, NEG))
    yield

    scale = hd ** -0.5 * LOG2_E
    out["q"] = jnp.concatenate([qt * scale] + bias_rows + [jnp.zeros((hd - nb, seq), F32)], axis=0).astype(BF16)
    yield
    k_blk = lax.broadcasted_iota(jnp.int32, (seq, hd), 0) // bs
    one_hot = jnp.where(k_blk == lax.broadcasted_iota(jnp.int32, (seq, hd), 1), 1.0, 0.0)
    out["k"] = jnp.concatenate([k, one_hot], axis=1).astype(BF16)
    yield
    vt = v_ref[...].astype(F32).T
    ones_rows = jnp.where(lax.broadcasted_iota(jnp.int32, (DENOM_ROWS, seq), 0) == 0, 1.0, 0.0)
    out["v"] = jnp.concatenate([vt, ones_rows], axis=0).astype(BF16)


def _moba_attend(operands, o_ref):
    q_aug, k_aug, v_aug = operands["q"], operands["k"], operands["v"]
    seq, hd = o_ref.shape
    bs = MOBA_BLOCK
    key = lax.broadcasted_iota(jnp.int32, (bs, bs), 0)
    qry = lax.broadcasted_iota(jnp.int32, (bs, bs), 1)
    for qb in range(seq // bs):
        qs = slice(qb * bs, (qb + 1) * bs)
        qq = q_aug[:, qs]
        logits = []
        for jb in range(qb + 1):
            lg = _dot(k_aug[jb * bs:(jb + 1) * bs], qq)
            if jb == qb:
                lg = jnp.where(key <= qry, lg, NEG)
            logits.append(lg)
        m = functools.reduce(jnp.maximum, [jnp.max(lg, axis=0, keepdims=True) for lg in logits])
        acc = functools.reduce(jnp.add, [_dot(v_aug[:, jb * bs:(jb + 1) * bs], jnp.exp2(lg - m).astype(BF16))
                                         for jb, lg in enumerate(logits)])
        o_ref[qs, :] = (acc[:hd] / acc[hd:hd + 1]).T.astype(o_ref.dtype)
        yield


def _moba(z, batch, seq):
    assert seq % MOBA_BLOCK == 0
    t = z.shape[0]
    mix = z.shape[1] // 4
    hd = mix // MOBA_HEADS
    cos, sin = _moba_tables(seq, hd)
    hps = MOBA_HEADS_PER_STEP
    groups = MOBA_HEADS // hps
    zspec = lambda part: pl.BlockSpec((seq, hps * hd), lambda b, g: (b, part * groups + g))
    table = pl.BlockSpec((seq, hd), lambda b, g: (0, 0))
    nb = seq // MOBA_BLOCK
    vmem = 2 * (3 * seq * hps * hd * 2 + 2 * seq * hd * 4 + seq * hps * hd * 2) \
        + hps * (10 * seq * hd * 4 + 3 * nb * MOBA_BLOCK * MOBA_BLOCK * 4)
    return pl.pallas_call(
        functools.partial(_moba_kernel, hd=hd),
        grid=(batch, groups),
        in_specs=[zspec(1), zspec(2), zspec(3), table, table],
        out_specs=pl.BlockSpec((seq, hps * hd), lambda b, g: (b, g)),
        out_shape=jax.ShapeDtypeStruct((t, mix), BF16),
        compiler_params=_params(("parallel", "parallel"), vmem),
        name="moba",
    )(z, z, z, cos, sin)


def kernel(x, p, norm_gains, w_ffn_gate, w_ffn_up, w_ffn_down, w_in_even, w_out_even, sgu_w, sgu_b, sgu_gain,
           w_in_odd, w_out_odd, pool_w, pool_scale, w_ple_gate, w_ple_proj):
    batch, seq, d = x.shape
    depth = norm_gains.shape[0]
    t = batch * seq
    assert seq % RET_CHUNK == 0 and RET_CHUNK == SGU_CHUNK and RET_HEADS == SGU_GROUPS

    gains = norm_gains.reshape(depth, norm_gains.shape[1], 1, d)
    wg, wu, wd = (w.astype(BF16) for w in (w_ffn_gate, w_ffn_up, w_ffn_down))
    w_in_e, w_out_e, w_in_o, w_out_o = (w.astype(BF16) for w in (w_in_even, w_out_even, w_in_odd, w_out_odd))
    wpg, wpp = w_ple_gate.astype(BF16), w_ple_proj.astype(BF16)
    pool_w16 = pool_w.astype(BF16)
    n_even = sgu_w.shape[0]
    sgu_b4 = sgu_b.reshape(n_even, SGU_GROUPS, SGU_CHUNK, 1)
    sgu_gain4 = sgu_gain.reshape(n_even, SGU_GROUPS, 1, -1)
    pool_scale4 = pool_scale.reshape(pool_scale.shape[0], len(POOL_WINDOWS), 1, -1)
    p3 = p.reshape(depth, t, p.shape[-1])

    xf = x.reshape(t, d)
    for i in range(depth):
        j = i // 2
        xf = _ffn(xf, gains, wg, wu, wd, i, 0, 0, 1)
        if i % 2 == 0:
            z = _norm_matmul(xf, gains, w_in_e, i, 2, j)
            ya, yb = _even_core(z, batch, seq, sgu_w, sgu_b4, sgu_gain4, j)
            xf = _out_proj(ya, yb, xf, gains, w_out_e, i, 3, j)
        else:
            z = _norm_matmul(xf, gains, w_in_o, i, 2, j)
            ya = _pool(z, batch, seq, pool_w16, pool_scale4, j)
            yb = _moba(z, batch, seq)
            xf = _out_proj(ya, yb, xf, gains, w_out_o, i, 3, j)
        xf = _ffn(xf, gains, wg, wu, wd, i, 1, 4, 5)
        xf = _ple(xf, p3, gains, wpg, wpp, i)
    return xf.reshape(batch, seq, d)
```

```python
import functools

import numpy as np
import jax
import jax.numpy as jnp
from jax import lax
from jax.experimental import pallas as pl
from jax.experimental.pallas import tpu as pltpu

F32 = jnp.float32
BF16 = jnp.bfloat16

RET_HEADS = 4
RET_CHUNK = 128
RET_THETA = 10000.0
SGU_GROUPS = 4
SGU_CHUNK = 128
POOL_WINDOWS = (2, 4, 8, 16)
MOBA_HEADS = 8
MOBA_BLOCK = 256
MOBA_TOPK = 3
MOBA_HEADS_PER_STEP = 4
ROPE_THETA = 500000.0
EPS = 1e-6
NEG = -1e30
LOG2_E = 1.4426950408889634
DENOM_ROWS = 16

V7X_VMEM_BYTES = 64 * 1024 * 1024
VMEM_REQUEST_CAP = V7X_VMEM_BYTES - 8 * 1024 * 1024

FFN_TOKENS = 1024
FFN_CHUNK = 512
FFN_ROWS = 512
PROJ_TOKENS = 1024
PROJ_COLS = 2048
ROW_TOKENS = 512


def _params(semantics, vmem_bytes):
    return pltpu.CompilerParams(dimension_semantics=semantics,
                                vmem_limit_bytes=int(min(vmem_bytes, VMEM_REQUEST_CAP)))


def _rms(x, gain=None):
    y = x * lax.rsqrt(jnp.mean(x * x, axis=-1, keepdims=True) + EPS)
    return y if gain is None else y * gain


def _dot(a, b):
    return jnp.dot(a, b, preferred_element_type=F32)


def _dot_nt(a, b, precision=None):
    return lax.dot_general(a, b, (((1,), (1,)), ((), ())), precision=precision, preferred_element_type=F32)


def _dot_tn(a, b):
    return lax.dot_general(a, b, (((0,), (0,)), ((), ())), preferred_element_type=F32)


def _ffn_kernel(x_ref, gin_ref, wg_ref, wu_ref, wd_ref, gout_ref, o_ref, h_ref, *, rows, nk):
    k = pl.program_id(1)
    tm = x_ref.shape[0]

    def step(first, last):
        for r in range(0, tm, rows):
            rs = slice(r, r + rows)
            if first:
                h = _rms(x_ref[rs, :], gin_ref[...]).astype(BF16)
                h_ref[rs, :] = h
            else:
                h = h_ref[rs, :]
            a = (jax.nn.silu(_dot(h, wg_ref[...])) * _dot(h, wu_ref[...])).astype(BF16)
            acc = _dot(a, wd_ref[...])
            if not first:
                acc = o_ref[rs, :] + acc
            if last:
                acc = x_ref[rs, :] + _rms(acc, 0.5 * gout_ref[...])
            o_ref[rs, :] = acc

    if nk == 1:
        step(True, True)
    else:
        pl.when(k == 0)(lambda: step(True, False))
        pl.when(jnp.logical_and(k > 0, k < nk - 1))(lambda: step(False, False))
        pl.when(k == nk - 1)(lambda: step(False, True))


def _ffn(x, gains, wg, wu, wd, layer, which, n_in, n_out):
    t, d = x.shape
    f = wg.shape[-1]
    tm = min(FFN_TOKENS, t)
    fc = min(FFN_CHUNK, f)
    rows = min(FFN_ROWS, tm)
    gain_spec = lambda n: pl.BlockSpec((None, None, 1, d), lambda i, k: (layer, n, 0, 0))
    vmem = (4 * tm * d * 4
            + 2 * 3 * d * fc * 2
            + tm * d * 2
            + 3 * rows * fc * 4
            + 2 * rows * d * 4)
    return pl.pallas_call(
        functools.partial(_ffn_kernel, rows=rows, nk=f // fc),
        grid=(t // tm, f // fc),
        in_specs=[
            pl.BlockSpec((tm, d), lambda i, k: (i, 0)),
            gain_spec(n_in),
            pl.BlockSpec((None, None, d, fc), lambda i, k: (layer, which, 0, k)),
            pl.BlockSpec((None, None, d, fc), lambda i, k: (layer, which, 0, k)),
            pl.BlockSpec((None, None, fc, d), lambda i, k: (layer, which, k, 0)),
            gain_spec(n_out),
        ],
        out_specs=pl.BlockSpec((tm, d), lambda i, k: (i, 0)),
        out_shape=jax.ShapeDtypeStruct((t, d), F32),
        scratch_shapes=[pltpu.VMEM((tm, d), BF16)],
        compiler_params=_params(("parallel", "arbitrary"), vmem),
        name="ffn",
    )(x, gains, wg, wu, wd, gains)


def _norm_matmul_kernel(x_ref, g_ref, w_ref, o_ref, h_ref, *, rows):
    def step(first):
        for r in range(0, x_ref.shape[0], rows):
            rs = slice(r, r + rows)
            if first:
                h = _rms(x_ref[rs, :], g_ref[...]).astype(BF16)
                h_ref[rs, :] = h
            else:
                h = h_ref[rs, :]
            o_ref[rs, :] = _dot(h, w_ref[...]).astype(o_ref.dtype)

    pl.when(pl.program_id(1) == 0)(lambda: step(True))
    pl.when(pl.program_id(1) > 0)(lambda: step(False))


def _norm_matmul(x, gains, w, layer, n_gain, j):
    t, d = x.shape
    n = w.shape[-1]
    tm = min(PROJ_TOKENS, t)
    tn = min(PROJ_COLS, n)
    vmem = 2 * tm * d * 4 + 2 * d * tn * 2 + 2 * tm * tn * 2 + tm * d * 2 + 2 * tm * d * 4 + tm * tn * 4
    return pl.pallas_call(
        functools.partial(_norm_matmul_kernel, rows=min(FFN_ROWS, tm)),
        grid=(t // tm, n // tn),
        in_specs=[
            pl.BlockSpec((tm, d), lambda i, k: (i, 0)),
            pl.BlockSpec((None, None, 1, d), lambda i, k: (layer, n_gain, 0, 0)),
            pl.BlockSpec((None, d, tn), lambda i, k: (j, 0, k)),
        ],
        out_specs=pl.BlockSpec((tm, tn), lambda i, k: (i, k)),
        out_shape=jax.ShapeDtypeStruct((t, n), BF16),
        scratch_shapes=[pltpu.VMEM((tm, d), BF16)],
        compiler_params=_params(("parallel", "arbitrary"), vmem),
        name="norm_in_proj",
    )(x, gains, w)


def _out_proj_kernel(ya_ref, yb_ref, x_ref, wa_ref, wb_ref, g_ref, o_ref):
    m = _dot(ya_ref[...], wa_ref[...]) + _dot(yb_ref[...], wb_ref[...])
    o_ref[...] = x_ref[...] + _rms(m, g_ref[...])


def _out_proj(ya, yb, x, gains, w, layer, n_gain, j):
    t, d = x.shape
    half = ya.shape[-1]
    tm = min(ROW_TOKENS, t)
    vmem = 2 * 2 * tm * half * 2 + 4 * tm * d * 4 + 2 * 2 * half * d * 2 + 3 * tm * d * 4
    return pl.pallas_call(
        _out_proj_kernel,
        grid=(t // tm,),
        in_specs=[
            pl.BlockSpec((tm, half), lambda i: (i, 0)),
            pl.BlockSpec((tm, half), lambda i: (i, 0)),
            pl.BlockSpec((tm, d), lambda i: (i, 0)),
            pl.BlockSpec((None, half, d), lambda i: (j, 0, 0)),
            pl.BlockSpec((None, half, d), lambda i: (j, 1, 0)),
            pl.BlockSpec((None, None, 1, d), lambda i: (layer, n_gain, 0, 0)),
        ],
        out_specs=pl.BlockSpec((tm, d), lambda i: (i, 0)),
        out_shape=jax.ShapeDtypeStruct((t, d), F32),
        compiler_params=_params(("parallel",), vmem),
        name="out_proj",
    )(ya, yb, x, w, w, gains)


def _ple_kernel(x_ref, p_ref, gin_ref, wgate_ref, wproj_ref, gout_ref, o_ref):
    x = x_ref[...]
    gate = jax.nn.sigmoid(_dot(_rms(x, gin_ref[...]).astype(BF16), wgate_ref[...]))
    emb = _dot(p_ref[...].astype(BF16), wproj_ref[...])
    o_ref[...] = x + _rms(gate * emb, gout_ref[...])


def _ple(x, p, gains, wgate, wproj, layer):
    t, d = x.shape
    e = p.shape[-1]
    tm = min(ROW_TOKENS, t)
    vmem = 4 * tm * d * 4 + 2 * tm * e * 4 + 2 * d * d * 2 + 2 * e * d * 2 + 4 * tm * d * 4
    return pl.pallas_call(
        _ple_kernel,
        grid=(t // tm,),
        in_specs=[
            pl.BlockSpec((tm, d), lambda i: (i, 0)),
            pl.BlockSpec((None, tm, e), lambda i: (layer, i, 0)),
            pl.BlockSpec((None, None, 1, d), lambda i: (layer, 6, 0, 0)),
            pl.BlockSpec((None, d, d), lambda i: (layer, 0, 0)),
            pl.BlockSpec((None, e, d), lambda i: (layer, 0, 0)),
            pl.BlockSpec((None, None, 1, d), lambda i: (layer, 7, 0, 0)),
        ],
        out_specs=pl.BlockSpec((tm, d), lambda i: (i, 0)),
        out_shape=jax.ShapeDtypeStruct((t, d), F32),
        compiler_params=_params(("parallel",), vmem),
        name="ple",
    )(x, p, gains, wgate, wproj, gains)


def _retention_tables(seq, hd):
    half = hd // 2
    inv = 1.0 / np.power(np.float32(RET_THETA), np.arange(half, dtype=np.float32) / half)
    ang = np.arange(seq, dtype=np.float32)[:, None] * inv[None, :]
    c = RET_CHUNK
    log_gamma = np.log(1.0 - np.power(2.0, -5.0 - np.arange(RET_HEADS, dtype=np.float64)))
    pos = np.arange(c, dtype=np.float64)
    rel = pos[:, None] - pos[None, :]
    intra = np.exp(np.maximum(rel, 0.0)[None] * log_gamma[:, None, None]) * (rel >= 0)[None]
    qd = np.broadcast_to(np.exp((pos + 1.0)[None, :] * log_gamma[:, None])[..., None], (RET_HEADS, c, hd))
    kd = np.broadcast_to(np.exp((c - 1.0 - pos)[None, :] * log_gamma[:, None])[..., None], (RET_HEADS, c, hd))
    cd = np.broadcast_to(np.exp(c * log_gamma)[:, None, None], (RET_HEADS, 1, hd))
    as32 = lambda a: jnp.asarray(np.ascontiguousarray(a), F32)
    return as32(np.cos(ang)), as32(np.sin(ang)), as32(intra), as32(qd), as32(kd), as32(cd)


def _even_core_kernel(q_ref, k_ref, v_ref, g_ref, u_ref, vs_ref, cos_ref, sin_ref, dec_ref, qd_ref, kd_ref,
                      cd_ref, ws_ref, bs_ref, gain_ref, oret_ref, osgu_ref):
    seq, hd = q_ref.shape
    half = hd // 2
    c = ws_ref.shape[-1]
    dec = dec_ref[...]
    qd = qd_ref[...]
    kd = kd_ref[...]
    cd = cd_ref[...]
    row = lax.broadcasted_iota(jnp.int32, (c, c), 0)
    col = lax.broadcasted_iota(jnp.int32, (c, c), 1)
    w = jnp.where(row >= col, ws_ref[...], 0.0).astype(BF16)
    bias = bs_ref[...]
    gain = gain_ref[...]

    state = jnp.zeros((hd, hd), F32)
    for i in range(seq // c):
        rs = slice(i * c, (i + 1) * c)
        cos = cos_ref[rs, :]
        sin = sin_ref[rs, :]

        def rotary(t):
            t1 = t[:, :half]
            t2 = t[:, half:]
            return jnp.concatenate([t1 * cos - t2 * sin, t2 * cos + t1 * sin], axis=-1)

        q = rotary(q_ref[rs, :].astype(F32))
        k = rotary(k_ref[rs, :].astype(F32)) * (hd ** -0.5)
        v = v_ref[rs, :]
        scores = _dot_nt(q.astype(BF16), k.astype(BF16)) * dec
        out = _dot(scores.astype(BF16), v) + _dot((q * qd).astype(BF16), state.astype(BF16))
        state = state * cd + _dot_tn((k * kd).astype(BF16), v)
        oret_ref[rs, :] = (_rms(out) * jax.nn.silu(g_ref[rs, :].astype(F32))).astype(oret_ref.dtype)

        vn = _rms(jax.nn.gelu(vs_ref[rs, :].astype(F32)), gain).astype(BF16)
        sv = _dot(w, vn) + bias
        osgu_ref[rs, :] = (jax.nn.gelu(u_ref[rs, :].astype(F32)) * sv).astype(osgu_ref.dtype)


def _even_core(z, batch, seq, sgu_w, sgu_b, sgu_gain, j):
    t = z.shape[0]
    mix = z.shape[1] // 6
    hd = mix // RET_HEADS
    c = RET_CHUNK
    cos, sin, intra, qd, kd, cd = _retention_tables(seq, hd)
    zspec = lambda part: pl.BlockSpec((seq, hd), lambda b, h: (b, part * RET_HEADS + h))
    table = pl.BlockSpec((seq, hd // 2), lambda b, h: (0, 0))
    head3 = lambda r, w: pl.BlockSpec((None, r, w), lambda b, h: (h, 0, 0))
    param = lambda r, w: pl.BlockSpec((None, None, r, w), lambda b, h: (j, h, 0, 0))
    out_spec = pl.BlockSpec((seq, hd), lambda b, h: (b, h))
    vmem = 2 * (6 * seq * hd * 4 + 2 * seq * (hd // 2) * 4 + 2 * seq * hd * 2) + 32 * c * hd * 4 + 16 * hd * hd * 4
    return pl.pallas_call(
        _even_core_kernel,
        grid=(batch, RET_HEADS),
        in_specs=[zspec(0), zspec(1), zspec(2), zspec(3), zspec(4), zspec(5), table, table,
                  head3(c, c), head3(c, hd), head3(c, hd), head3(1, hd),
                  param(c, c), param(c, 1), param(1, hd)],
        out_specs=[out_spec, out_spec],
        out_shape=[jax.ShapeDtypeStruct((t, mix), BF16), jax.ShapeDtypeStruct((t, mix), BF16)],
        compiler_params=_params(("parallel", "parallel"), vmem),
        name="even_core",
    )(z, z, z, z, z, z, cos, sin, intra, qd, kd, cd, sgu_w, sgu_b, sgu_gain)


def _pool_kernel(z_ref, w_ref, scale_ref, o_ref):
    seq = z_ref.shape[0]
    dg = w_ref.shape[-1]
    t = lax.broadcasted_iota(jnp.int32, (seq, dg), 0)
    for gi, width in enumerate(POOL_WINDOWS):
        ls = slice(gi * dg, (gi + 1) * dg)
        z = z_ref[:, ls].astype(F32)
        s = z
        sh = 1
        while sh < width:
            s = s + jnp.where(t >= sh, pltpu.roll(s, sh, 0), 0.0)
            sh *= 2
        cnt = jnp.minimum(t + 1, width).astype(F32)
        y = s / cnt - z
        o_ref[:, ls] = (_dot(y.astype(BF16), w_ref[gi]) * scale_ref[gi]).astype(o_ref.dtype)


def _pool(z, batch, seq, pool_w, pool_scale, j):
    assert all(w == POOL_WINDOWS[0] << i for i, w in enumerate(POOL_WINDOWS))
    t = z.shape[0]
    mix = z.shape[1] // 4
    ng = len(POOL_WINDOWS)
    dg = mix // ng
    vmem = 2 * (2 * seq * mix * 2 + ng * dg * dg * 2) + ng * 6 * seq * dg * 4
    return pl.pallas_call(
        _pool_kernel,
        grid=(batch,),
        in_specs=[pl.BlockSpec((seq, mix), lambda b: (b, 0)),
                  pl.BlockSpec((None, ng, dg, dg), lambda b: (j, 0, 0, 0)),
                  pl.BlockSpec((None, ng, 1, dg), lambda b: (j, 0, 0, 0))],
        out_specs=pl.BlockSpec((seq, mix), lambda b: (b, 0)),
        out_shape=jax.ShapeDtypeStruct((t, mix), BF16),
        compiler_params=_params(("parallel",), vmem),
        name="pool",
    )(z, pool_w, pool_scale)


def _moba_tables(seq, hd):
    rot = hd // 4
    half = rot // 2
    inv = 1.0 / np.power(np.float32(ROPE_THETA), np.arange(half, dtype=np.float32) / half)
    ang = np.arange(seq, dtype=np.float32)[:, None] * inv[None, :]
    cos = np.concatenate([np.cos(ang), np.cos(ang), np.ones((seq, hd - rot), np.float32)], axis=1)
    sin = np.concatenate([-np.sin(ang), np.sin(ang), np.zeros((seq, hd - rot), np.float32)], axis=1)
    return jnp.asarray(cos, F32), jnp.asarray(sin, F32)


def _moba_kernel(q_ref, k_ref, v_ref, cos_ref, sin_ref, o_ref, *, hd):
    views = [[r.at[:, pl.ds(head * hd, hd)] for r in (q_ref, k_ref, v_ref, o_ref)]
             for head in range(q_ref.shape[1] // hd)]
    operands = [{} for _ in views]
    prepare = [_moba_prepare(v[0], v[1], v[2], cos_ref, sin_ref, out) for v, out in zip(views, operands)]
    for _ in prepare[0]:
        pass
    for head, view in enumerate(views):
        upcoming = prepare[head + 1] if head + 1 < len(views) else iter(())
        for _ in _moba_attend(operands[head], view[3]):
            next(upcoming, None)
        for _ in upcoming:
            pass


def _moba_prepare(q_ref, k_ref, v_ref, cos_ref, sin_ref, out):
    seq, hd = q_ref.shape
    bs = MOBA_BLOCK
    nb = seq // bs
    n_sel = min(MOBA_TOPK, nb)
    half = hd // 8
    lane = lax.broadcasted_iota(jnp.int32, (seq, hd), 1)

    def rotary(t):
        partner = jnp.where(lane < half, pltpu.roll(t, hd - half, 1), pltpu.roll(t, half, 1))
        return t * cos_ref[...] + partner * sin_ref[...]

    q = rotary(q_ref[...].astype(F32))
    yield
    k = rotary(k_ref[...].astype(F32))
    k_mean = jnp.mean(k.reshape(nb, bs, hd), axis=1)
    yield
    qt = q.T
    gate = jnp.dot(k_mean, qt, precision=lax.Precision.HIGHEST, preferred_element_type=F32)
    yield

    q_blk = lax.broadcasted_iota(jnp.int32, (1, seq), 1) // bs
    bias_rows = []
    for n in range(nb):
        mine = gate[n:n + 1]
        rank = jnp.zeros((1, seq), F32)
        for o in range(nb):
            if o != n:
                other = gate[o:o + 1]
                ahead = (other >= mine) if o < n else (other > mine)
                rank = rank + jnp.where(jnp.logical_and(ahead, q_blk > o), 1.0, 0.0)
        keep = jnp.logical_or(rank < n_sel, q_blk <= n)
        bias_rows.append(jnp.where(keep, 0.0, NEG))
    yield

    scale = hd ** -0.5 * LOG2_E
    out["q"] = jnp.concatenate([qt * scale] + bias_rows + [jnp.zeros((hd - nb, seq), F32)], axis=0).astype(BF16)
    yield
    k_blk = lax.broadcasted_iota(jnp.int32, (seq, hd), 0) // bs
    one_hot = jnp.where(k_blk == lax.broadcasted_iota(jnp.int32, (seq, hd), 1), 1.0, 0.0)
    out["k"] = jnp.concatenate([k, one_hot], axis=1).astype(BF16)
    yield
    vt = v_ref[...].astype(F32).T
    ones_rows = jnp.where(lax.broadcasted_iota(jnp.int32, (DENOM_ROWS, seq), 0) == 0, 1.0, 0.0)
    out["v"] = jnp.concatenate([vt, ones_rows], axis=0).astype(BF16)


def _moba_attend(operands, o_ref):
    q_aug, k_aug, v_aug = operands["q"], operands["k"], operands["v"]
    seq, hd = o_ref.shape
    bs = MOBA_BLOCK
    key = lax.broadcasted_iota(jnp.int32, (bs, bs), 0)
    qry = lax.broadcasted_iota(jnp.int32, (bs, bs), 1)
    for qb in range(seq // bs):
        qs = slice(qb * bs, (qb + 1) * bs)
        qq = q_aug[:, qs]
        logits = []
        for jb in range(qb + 1):
            lg = _dot(k_aug[jb * bs:(jb + 1) * bs], qq)
            if jb == qb:
                lg = jnp.where(key <= qry, lg, NEG)
            logits.append(lg)
        m = functools.reduce(jnp.maximum, [jnp.max(lg, axis=0, keepdims=True) for lg in logits])
        acc = functools.reduce(jnp.add, [_dot(v_aug[:, jb * bs:(jb + 1) * bs], jnp.exp2(lg - m).astype(BF16))
                                         for jb, lg in enumerate(logits)])
        o_ref[qs, :] = (acc[:hd] / acc[hd:hd + 1]).T.astype(o_ref.dtype)
        yield


def _moba(z, batch, seq):
    assert seq % MOBA_BLOCK == 0
    t = z.shape[0]
    mix = z.shape[1] // 4
    hd = mix // MOBA_HEADS
    cos, sin = _moba_tables(seq, hd)
    hps = MOBA_HEADS_PER_STEP
    groups = MOBA_HEADS // hps
    zspec = lambda part: pl.BlockSpec((seq, hps * hd), lambda b, g: (b, part * groups + g))
    table = pl.BlockSpec((seq, hd), lambda b, g: (0, 0))
    nb = seq // MOBA_BLOCK
    vmem = 2 * (3 * seq * hps * hd * 2 + 2 * seq * hd * 4 + seq * hps * hd * 2) \
        + hps * (10 * seq * hd * 4 + 3 * nb * MOBA_BLOCK * MOBA_BLOCK * 4)
    return pl.pallas_call(
        functools.partial(_moba_kernel, hd=hd),
        grid=(batch, groups),
        in_specs=[zspec(1), zspec(2), zspec(3), table, table],
        out_specs=pl.BlockSpec((seq, hps * hd), lambda b, g: (b, g)),
        out_shape=jax.ShapeDtypeStruct((t, mix), BF16),
        compiler_params=_params(("parallel", "parallel"), vmem),
        name="moba",
    )(z, z, z, cos, sin)


def kernel(x, p, norm_gains, w_ffn_gate, w_ffn_up, w_ffn_down, w_in_even, w_out_even, sgu_w, sgu_b, sgu_gain,
           w_in_odd, w_out_odd, pool_w, pool_scale, w_ple_gate, w_ple_proj):
    batch, seq, d = x.shape
    depth = norm_gains.shape[0]
    t = batch * seq
    assert seq % RET_CHUNK == 0 and RET_CHUNK == SGU_CHUNK and RET_HEADS == SGU_GROUPS

    gains = norm_gains.reshape(depth, norm_gains.shape[1], 1, d)
    wg, wu, wd = (w.astype(BF16) for w in (w_ffn_gate, w_ffn_up, w_ffn_down))
    w_in_e, w_out_e, w_in_o, w_out_o = (w.astype(BF16) for w in (w_in_even, w_out_even, w_in_odd, w_out_odd))
    wpg, wpp = w_ple_gate.astype(BF16), w_ple_proj.astype(BF16)
    pool_w16 = pool_w.astype(BF16)
    n_even = sgu_w.shape[0]
    sgu_b4 = sgu_b.reshape(n_even, SGU_GROUPS, SGU_CHUNK, 1)
    sgu_gain4 = sgu_gain.reshape(n_even, SGU_GROUPS, 1, -1)
    pool_scale4 = pool_scale.reshape(pool_scale.shape[0], len(POOL_WINDOWS), 1, -1)
    p3 = p.reshape(depth, t, p.shape[-1])

    xf = x.reshape(t, d)
    for i in range(depth):
        j = i // 2
        xf = _ffn(xf, gains, wg, wu, wd, i, 0, 0, 1)
        if i % 2 == 0:
            z = _norm_matmul(xf, gains, w_in_e, i, 2, j)
            ya, yb = _even_core(z, batch, seq, sgu_w, sgu_b4, sgu_gain4, j)
            xf = _out_proj(ya, yb, xf, gains, w_out_e, i, 3, j)
        else:
            z = _norm_matmul(xf, gains, w_in_o, i, 2, j)
            ya = _pool(z, batch, seq, pool_w16, pool_scale4, j)
            yb = _moba(z, batch, seq)
            xf = _out_proj(ya, yb, xf, gains, w_out_o, i, 3, j)
        xf = _ffn(xf, gains, wg, wu, wd, i, 1, 4, 5)
        xf = _ple(xf, p3, gains, wpg, wpp, i)
    return xf.reshape(batch, seq, d)
```

```python
import functools

import numpy as np
import jax
import jax.numpy as jnp
from jax import lax
from jax.experimental import pallas as pl
from jax.experimental.pallas import tpu as pltpu

F32 = jnp.float32
BF16 = jnp.bfloat16

RET_HEADS = 4
RET_CHUNK = 128
RET_THETA = 10000.0
SGU_GROUPS = 4
SGU_CHUNK = 128
POOL_WINDOWS = (2, 4, 8, 16)
MOBA_HEADS = 8
MOBA_BLOCK = 256
MOBA_TOPK = 3
MOBA_HEADS_PER_STEP = 4
ROPE_THETA = 500000.0
EPS = 1e-6
NEG = -1e30
LOG2_E = 1.4426950408889634
DENOM_ROWS = 16

V7X_VMEM_BYTES = 64 * 1024 * 1024
VMEM_REQUEST_CAP = V7X_VMEM_BYTES - 8 * 1024 * 1024

FFN_TOKENS = 1024
FFN_CHUNK = 512
FFN_ROWS = 512
PROJ_TOKENS = 1024
PROJ_COLS = 2048
ROW_TOKENS = 512


def _params(semantics, vmem_bytes):
    return pltpu.CompilerParams(dimension_semantics=semantics,
                                vmem_limit_bytes=int(min(vmem_bytes, VMEM_REQUEST_CAP)))


def _rms(x, gain=None):
    y = x * lax.rsqrt(jnp.mean(x * x, axis=-1, keepdims=True) + EPS)
    return y if gain is None else y * gain


def _dot(a, b):
    return jnp.dot(a, b, preferred_element_type=F32)


def _dot_nt(a, b, precision=None):
    return lax.dot_general(a, b, (((1,), (1,)), ((), ())), precision=precision, preferred_element_type=F32)


def _dot_tn(a, b):
    return lax.dot_general(a, b, (((0,), (0,)), ((), ())), preferred_element_type=F32)


def _ffn_kernel(x_ref, gin_ref, wg_ref, wu_ref, wd_ref, gout_ref, o_ref, h_ref, *, rows, nk):
    k = pl.program_id(1)
    tm = x_ref.shape[0]

    def step(first, last):
        for r in range(0, tm, rows):
            rs = slice(r, r + rows)
            if first:
                h = _rms(x_ref[rs, :], gin_ref[...]).astype(BF16)
                h_ref[rs, :] = h
            else:
                h = h_ref[rs, :]
            a = (jax.nn.silu(_dot(h, wg_ref[...])) * _dot(h, wu_ref[...])).astype(BF16)
            acc = _dot(a, wd_ref[...])
            if not first:
                acc = o_ref[rs, :] + acc
            if last:
                acc = x_ref[rs, :] + _rms(acc, 0.5 * gout_ref[...])
            o_ref[rs, :] = acc

    if nk == 1:
        step(True, True)
    else:
        pl.when(k == 0)(lambda: step(True, False))
        pl.when(jnp.logical_and(k > 0, k < nk - 1))(lambda: step(False, False))
        pl.when(k == nk - 1)(lambda: step(False, True))


def _ffn(x, gains, wg, wu, wd, layer, which, n_in, n_out):
    t, d = x.shape
    f = wg.shape[-1]
    tm = min(FFN_TOKENS, t)
    fc = min(FFN_CHUNK, f)
    rows = min(FFN_ROWS, tm)
    gain_spec = lambda n: pl.BlockSpec((None, None, 1, d), lambda i, k: (layer, n, 0, 0))
    vmem = (4 * tm * d * 4
            + 2 * 3 * d * fc * 2
            + tm * d * 2
            + 3 * rows * fc * 4
            + 2 * rows * d * 4)
    return pl.pallas_call(
        functools.partial(_ffn_kernel, rows=rows, nk=f // fc),
        grid=(t // tm, f // fc),
        in_specs=[
            pl.BlockSpec((tm, d), lambda i, k: (i, 0)),
            gain_spec(n_in),
            pl.BlockSpec((None, None, d, fc), lambda i, k: (layer, which, 0, k)),
            pl.BlockSpec((None, None, d, fc), lambda i, k: (layer, which, 0, k)),
            pl.BlockSpec((None, None, fc, d), lambda i, k: (layer, which, k, 0)),
            gain_spec(n_out),
        ],
        out_specs=pl.BlockSpec((tm, d), lambda i, k: (i, 0)),
        out_shape=jax.ShapeDtypeStruct((t, d), F32),
        scratch_shapes=[pltpu.VMEM((tm, d), BF16)],
        compiler_params=_params(("parallel", "arbitrary"), vmem),
        name="ffn",
    )(x, gains, wg, wu, wd, gains)


def _norm_matmul_kernel(x_ref, g_ref, w_ref, o_ref, h_ref, *, rows):
    def step(first):
        for r in range(0, x_ref.shape[0], rows):
            rs = slice(r, r + rows)
            if first:
                h = _rms(x_ref[rs, :], g_ref[...]).astype(BF16)
                h_ref[rs, :] = h
            else:
                h = h_ref[rs, :]
            o_ref[rs, :] = _dot(h, w_ref[...]).astype(o_ref.dtype)

    pl.when(pl.program_id(1) == 0)(lambda: step(True))
    pl.when(pl.program_id(1) > 0)(lambda: step(False))


def _norm_matmul(x, gains, w, layer, n_gain, j):
    t, d = x.shape
    n = w.shape[-1]
    tm = min(PROJ_TOKENS, t)
    tn = min(PROJ_COLS, n)
    vmem = 2 * tm * d * 4 + 2 * d * tn * 2 + 2 * tm * tn * 2 + tm * d * 2 + 2 * tm * d * 4 + tm * tn * 4
    return pl.pallas_call(
        functools.partial(_norm_matmul_kernel, rows=min(FFN_ROWS, tm)),
        grid=(t // tm, n // tn),
        in_specs=[
            pl.BlockSpec((tm, d), lambda i, k: (i, 0)),
            pl.BlockSpec((None, None, 1, d), lambda i, k: (layer, n_gain, 0, 0)),
            pl.BlockSpec((None, d, tn), lambda i, k: (j, 0, k)),
        ],
        out_specs=pl.BlockSpec((tm, tn), lambda i, k: (i, k)),
        out_shape=jax.ShapeDtypeStruct((t, n), BF16),
        scratch_shapes=[pltpu.VMEM((tm, d), BF16)],
        compiler_params=_params(("parallel", "arbitrary"), vmem),
        name="norm_in_proj",
    )(x, gains, w)


def _out_proj_kernel(ya_ref, yb_ref, x_ref, wa_ref, wb_ref, g_ref, o_ref):
    m = _dot(ya_ref[...], wa_ref[...]) + _dot(yb_ref[...], wb_ref[...])
    o_ref[...] = x_ref[...] + _rms(m, g_ref[...])


def _out_proj(ya, yb, x, gains, w, layer, n_gain, j):
    t, d = x.shape
    half = ya.shape[-1]
    tm = min(ROW_TOKENS, t)
    vmem = 2 * 2 * tm * half * 2 + 4 * tm * d * 4 + 2 * 2 * half * d * 2 + 3 * tm * d * 4
    return pl.pallas_call(
        _out_proj_kernel,
        grid=(t // tm,),
        in_specs=[
            pl.BlockSpec((tm, half), lambda i: (i, 0)),
            pl.BlockSpec((tm, half), lambda i: (i, 0)),
            pl.BlockSpec((tm, d), lambda i: (i, 0)),
            pl.BlockSpec((None, half, d), lambda i: (j, 0, 0)),
            pl.BlockSpec((None, half, d), lambda i: (j, 1, 0)),
            pl.BlockSpec((None, None, 1, d), lambda i: (layer, n_gain, 0, 0)),
        ],
        out_specs=pl.BlockSpec((tm, d), lambda i: (i, 0)),
        out_shape=jax.ShapeDtypeStruct((t, d), F32),
        compiler_params=_params(("parallel",), vmem),
        name="out_proj",
    )(ya, yb, x, w, w, gains)


def _ple_kernel(x_ref, p_ref, gin_ref, wgate_ref, wproj_ref, gout_ref, o_ref):
    x = x_ref[...]
    gate = jax.nn.sigmoid(_dot(_rms(x, gin_ref[...]).astype(BF16), wgate_ref[...]))
    emb = _dot(p_ref[...].astype(BF16), wproj_ref[...])
    o_ref[...] = x + _rms(gate * emb, gout_ref[...])


def _ple(x, p, gains, wgate, wproj, layer):
    t, d = x.shape
    e = p.shape[-1]
    tm = min(ROW_TOKENS, t)
    vmem = 4 * tm * d * 4 + 2 * tm * e * 4 + 2 * d * d * 2 + 2 * e * d * 2 + 4 * tm * d * 4
    return pl.pallas_call(
        _ple_kernel,
        grid=(t // tm,),
        in_specs=[
            pl.BlockSpec((tm, d), lambda i: (i, 0)),
            pl.BlockSpec((None, tm, e), lambda i: (layer, i, 0)),
            pl.BlockSpec((None, None, 1, d), lambda i: (layer, 6, 0, 0)),
            pl.BlockSpec((None, d, d), lambda i: (layer, 0, 0)),
            pl.BlockSpec((None, e, d), lambda i: (layer, 0, 0)),
            pl.BlockSpec((None, None, 1, d), lambda i: (layer, 7, 0, 0)),
        ],
        out_specs=pl.BlockSpec((tm, d), lambda i: (i, 0)),
        out_shape=jax.ShapeDtypeStruct((t, d), F32),
        compiler_params=_params(("parallel",), vmem),
        name="ple",
    )(x, p, gains, wgate, wproj, gains)


def _retention_tables(seq, hd):
    half = hd // 2
    inv = 1.0 / np.power(np.float32(RET_THETA), np.arange(half, dtype=np.float32) / half)
    ang = np.arange(seq, dtype=np.float32)[:, None] * inv[None, :]
    c = RET_CHUNK
    log_gamma = np.log(1.0 - np.power(2.0, -5.0 - np.arange(RET_HEADS, dtype=np.float64)))
    pos = np.arange(c, dtype=np.float64)
    rel = pos[:, None] - pos[None, :]
    intra = np.exp(np.maximum(rel, 0.0)[None] * log_gamma[:, None, None]) * (rel >= 0)[None]
    qd = np.broadcast_to(np.exp((pos + 1.0)[None, :] * log_gamma[:, None])[..., None], (RET_HEADS, c, hd))
    kd = np.broadcast_to(np.exp((c - 1.0 - pos)[None, :] * log_gamma[:, None])[..., None], (RET_HEADS, c, hd))
    cd = np.broadcast_to(np.exp(c * log_gamma)[:, None, None], (RET_HEADS, 1, hd))
    as32 = lambda a: jnp.asarray(np.ascontiguousarray(a), F32)
    return as32(np.cos(ang)), as32(np.sin(ang)), as32(intra), as32(qd), as32(kd), as32(cd)


def _even_core_kernel(q_ref, k_ref, v_ref, g_ref, u_ref, vs_ref, cos_ref, sin_ref, dec_ref, qd_ref, kd_ref,
                      cd_ref, ws_ref, bs_ref, gain_ref, oret_ref, osgu_ref):
    seq, hd = q_ref.shape
    half = hd // 2
    c = ws_ref.shape[-1]
    dec = dec_ref[...]
    qd = qd_ref[...]
    kd = kd_ref[...]
    cd = cd_ref[...]
    row = lax.broadcasted_iota(jnp.int32, (c, c), 0)
    col = lax.broadcasted_iota(jnp.int32, (c, c), 1)
    w = jnp.where(row >= col, ws_ref[...], 0.0).astype(BF16)
    bias = bs_ref[...]
    gain = gain_ref[...]

    state = jnp.zeros((hd, hd), F32)
    for i in range(seq // c):
        rs = slice(i * c, (i + 1) * c)
        cos = cos_ref[rs, :]
        sin = sin_ref[rs, :]

        def rotary(t):
            t1 = t[:, :half]
            t2 = t[:, half:]
            return jnp.concatenate([t1 * cos - t2 * sin, t2 * cos + t1 * sin], axis=-1)

        q = rotary(q_ref[rs, :].astype(F32))
        k = rotary(k_ref[rs, :].astype(F32)) * (hd ** -0.5)
        v = v_ref[rs, :]
        scores = _dot_nt(q.astype(BF16), k.astype(BF16)) * dec
        out = _dot(scores.astype(BF16), v) + _dot((q * qd).astype(BF16), state.astype(BF16))
        state = state * cd + _dot_tn((k * kd).astype(BF16), v)
        oret_ref[rs, :] = (_rms(out) * jax.nn.silu(g_ref[rs, :].astype(F32))).astype(oret_ref.dtype)

        vn = _rms(jax.nn.gelu(vs_ref[rs, :].astype(F32)), gain).astype(BF16)
        sv = _dot(w, vn) + bias
        osgu_ref[rs, :] = (jax.nn.gelu(u_ref[rs, :].astype(F32)) * sv).astype(osgu_ref.dtype)


def _even_core(z, batch, seq, sgu_w, sgu_b, sgu_gain, j):
    t = z.shape[0]
    mix = z.shape[1] // 6
    hd = mix // RET_HEADS
    c = RET_CHUNK
    cos, sin, intra, qd, kd, cd = _retention_tables(seq, hd)
    zspec = lambda part: pl.BlockSpec((seq, hd), lambda b, h: (b, part * RET_HEADS + h))
    table = pl.BlockSpec((seq, hd // 2), lambda b, h: (0, 0))
    head3 = lambda r, w: pl.BlockSpec((None, r, w), lambda b, h: (h, 0, 0))
    param = lambda r, w: pl.BlockSpec((None, None, r, w), lambda b, h: (j, h, 0, 0))
    out_spec = pl.BlockSpec((seq, hd), lambda b, h: (b, h))
    vmem = 2 * (6 * seq * hd * 4 + 2 * seq * (hd // 2) * 4 + 2 * seq * hd * 2) + 32 * c * hd * 4 + 16 * hd * hd * 4
    return pl.pallas_call(
        _even_core_kernel,
        grid=(batch, RET_HEADS),
        in_specs=[zspec(0), zspec(1), zspec(2), zspec(3), zspec(4), zspec(5), table, table,
                  head3(c, c), head3(c, hd), head3(c, hd), head3(1, hd),
                  param(c, c), param(c, 1), param(1, hd)],
        out_specs=[out_spec, out_spec],
        out_shape=[jax.ShapeDtypeStruct((t, mix), BF16), jax.ShapeDtypeStruct((t, mix), BF16)],
        compiler_params=_params(("parallel", "parallel"), vmem),
        name="even_core",
    )(z, z, z, z, z, z, cos, sin, intra, qd, kd, cd, sgu_w, sgu_b, sgu_gain)


def _pool_kernel(z_ref, w_ref, scale_ref, o_ref):
    seq = z_ref.shape[0]
    dg = w_ref.shape[-1]
    t = lax.broadcasted_iota(jnp.int32, (seq, dg), 0)
    for gi, width in enumerate(POOL_WINDOWS):
        ls = slice(gi * dg, (gi + 1) * dg)
        z = z_ref[:, ls].astype(F32)
        s = z
        sh = 1
        while sh < width:
            s = s + jnp.where(t >= sh, pltpu.roll(s, sh, 0), 0.0)
            sh *= 2
        cnt = jnp.minimum(t + 1, width).astype(F32)
        y = s / cnt - z
        o_ref[:, ls] = (_dot(y.astype(BF16), w_ref[gi]) * scale_ref[gi]).astype(o_ref.dtype)


def _pool(z, batch, seq, pool_w, pool_scale, j):
    assert all(w == POOL_WINDOWS[0] << i for i, w in enumerate(POOL_WINDOWS))
    t = z.shape[0]
    mix = z.shape[1] // 4
    ng = len(POOL_WINDOWS)
    dg = mix // ng
    vmem = 2 * (2 * seq * mix * 2 + ng * dg * dg * 2) + ng * 6 * seq * dg * 4
    return pl.pallas_call(
        _pool_kernel,
        grid=(batch,),
        in_specs=[pl.BlockSpec((seq, mix), lambda b: (b, 0)),
                  pl.BlockSpec((None, ng, dg, dg), lambda b: (j, 0, 0, 0)),
                  pl.BlockSpec((None, ng, 1, dg), lambda b: (j, 0, 0, 0))],
        out_specs=pl.BlockSpec((seq, mix), lambda b: (b, 0)),
        out_shape=jax.ShapeDtypeStruct((t, mix), BF16),
        compiler_params=_params(("parallel",), vmem),
        name="pool",
    )(z, pool_w, pool_scale)


def _moba_tables(seq, hd):
    rot = hd // 4
    half = rot // 2
    inv = 1.0 / np.power(np.float32(ROPE_THETA), np.arange(half, dtype=np.float32) / half)
    ang = np.arange(seq, dtype=np.float32)[:, None] * inv[None, :]
    cos = np.concatenate([np.cos(ang), np.cos(ang), np.ones((seq, hd - rot), np.float32)], axis=1)
    sin = np.concatenate([-np.sin(ang), np.sin(ang), np.zeros((seq, hd - rot), np.float32)], axis=1)
    return jnp.asarray(cos, F32), jnp.asarray(sin, F32)


def _moba_kernel(q_ref, k_ref, v_ref, cos_ref, sin_ref, o_ref, *scratch, hd):
    heads = q_ref.shape[1] // hd
    views = [[r.at[:, pl.ds(head * hd, hd)] for r in (q_ref, k_ref, v_ref, o_ref)] for head in range(heads)]
    operands = [scratch[3 * head:3 * head + 3] for head in range(heads)]
    prepare = [_moba_prepare(v[0], v[1], v[2], cos_ref, sin_ref, *ops) for v, ops in zip(views, operands)]
    for _ in prepare[0]:
        pass
    for head, view in enumerate(views):
        upcoming = prepare[head + 1] if head + 1 < heads else iter(())
        for _ in _moba_attend(*operands[head], view[3]):
            next(upcoming, None)
        for _ in upcoming:
            pass


def _moba_prepare(q_ref, k_ref, v_ref, cos_ref, sin_ref, qa_ref, ka_ref, va_ref):
    seq, hd = q_ref.shape
    bs = MOBA_BLOCK
    nb = seq // bs
    n_sel = min(MOBA_TOPK, nb)
    half = hd // 8
    lane = lax.broadcasted_iota(jnp.int32, (bs, hd), 1)
    ones_rows = jnp.where(lax.broadcasted_iota(jnp.int32, (DENOM_ROWS, bs), 0) == 0, 1.0, 0.0)
    scale = hd ** -0.5 * LOG2_E

    def rotary(t, rs):
        partner = jnp.where(lane < half, pltpu.roll(t, hd - half, 1), pltpu.roll(t, half, 1))
        return t * cos_ref[rs, :] + partner * sin_ref[rs, :]

    k_means = []
    for j in range(nb):
        rs = slice(j * bs, (j + 1) * bs)
        k = rotary(k_ref[rs, :].astype(F32), rs)
        k_means.append(jnp.mean(k, axis=0, keepdims=True))
        ka_ref[rs, :] = jnp.concatenate([k, jnp.where(lane == j, 1.0, 0.0)], axis=1).astype(BF16)
        va_ref[:, rs] = jnp.concatenate([v_ref[rs, :].astype(F32).T, ones_rows], axis=0).astype(BF16)
        if j % 2:
            yield
    k_mean = jnp.concatenate(k_means, axis=0)

    for j in range(nb):
        rs = slice(j * bs, (j + 1) * bs)
        qt = rotary(q_ref[rs, :].astype(F32), rs).T
        gate = jnp.dot(k_mean, qt, precision=lax.Precision.HIGHEST, preferred_element_type=F32)
        rows = []
        for n in range(nb):
            if n >= j or j <= n_sel:
                rows.append(jnp.zeros((1, bs), F32))
                continue
            mine = gate[n:n + 1]
            rank = jnp.zeros((1, bs), F32)
            for o in range(j):
                if o != n:
                    other = gate[o:o + 1]
                    rank = rank + jnp.where((other >= mine) if o < n else (other > mine), 1.0, 0.0)
            rows.append(jnp.where(rank < n_sel, 0.0, NEG))
        qa_ref[:, rs] = jnp.concatenate([qt * scale] + rows + [jnp.zeros((hd - nb, bs), F32)],
                                        axis=0).astype(BF16)
        if j % 2:
            yield


def _moba_attend(qa_ref, ka_ref, va_ref, o_ref):
    seq, hd = o_ref.shape
    bs = MOBA_BLOCK
    key = lax.broadcasted_iota(jnp.int32, (bs, bs), 0)
    qry = lax.broadcasted_iota(jnp.int32, (bs, bs), 1)
    for qb in range(seq // bs):
        qs = slice(qb * bs, (qb + 1) * bs)
        qq = qa_ref[:, qs]
        logits = []
        for jb in range(qb + 1):
            lg = _dot(ka_ref[jb * bs:(jb + 1) * bs, :], qq)
            if jb == qb:
                lg = jnp.where(key <= qry, lg, NEG)
            logits.append(lg)
        m = functools.reduce(jnp.maximum, [jnp.max(lg, axis=0, keepdims=True) for lg in logits])
        acc = functools.reduce(jnp.add, [_dot(va_ref[:, jb * bs:(jb + 1) * bs], jnp.exp2(lg - m).astype(BF16))
                                         for jb, lg in enumerate(logits)])
        o_ref[qs, :] = (acc[:hd] / acc[hd:hd + 1]).T.astype(o_ref.dtype)
        yield


def _moba(z, batch, seq):
    assert seq % MOBA_BLOCK == 0
    t = z.shape[0]
    mix = z.shape[1] // 4
    hd = mix // MOBA_HEADS
    cos, sin = _moba_tables(seq, hd)
    hps = MOBA_HEADS_PER_STEP
    groups = MOBA_HEADS // hps
    zspec = lambda part: pl.BlockSpec((seq, hps * hd), lambda b, g: (b, part * groups + g))
    table = pl.BlockSpec((seq, hd), lambda b, g: (0, 0))
    nb = seq // MOBA_BLOCK
    vmem = 2 * (3 * seq * hps * hd * 2 + 2 * seq * hd * 4 + seq * hps * hd * 2) \
        + hps * (10 * seq * hd * 4 + 3 * nb * MOBA_BLOCK * MOBA_BLOCK * 4)
    return pl.pallas_call(
        functools.partial(_moba_kernel, hd=hd),
        grid=(batch, groups),
        in_specs=[zspec(1), zspec(2), zspec(3), table, table],
        out_specs=pl.BlockSpec((seq, hps * hd), lambda b, g: (b, g)),
        out_shape=jax.ShapeDtypeStruct((t, mix), BF16),
        scratch_shapes=[pltpu.VMEM(shape, BF16) for _ in range(hps)
                        for shape in ((2 * hd, seq), (seq, 2 * hd), (hd + DENOM_ROWS, seq))],
        compiler_params=_params(("parallel", "parallel"), vmem),
        name="moba",
    )(z, z, z, cos, sin)


def kernel(x, p, norm_gains, w_ffn_gate, w_ffn_up, w_ffn_down, w_in_even, w_out_even, sgu_w, sgu_b, sgu_gain,
           w_in_odd, w_out_odd, pool_w, pool_scale, w_ple_gate, w_ple_proj):
    batch, seq, d = x.shape
    depth = norm_gains.shape[0]
    t = batch * seq
    assert seq % RET_CHUNK == 0 and RET_CHUNK == SGU_CHUNK and RET_HEADS == SGU_GROUPS

    gains = norm_gains.reshape(depth, norm_gains.shape[1], 1, d)
    wg, wu, wd = (w.astype(BF16) for w in (w_ffn_gate, w_ffn_up, w_ffn_down))
    w_in_e, w_out_e, w_in_o, w_out_o = (w.astype(BF16) for w in (w_in_even, w_out_even, w_in_odd, w_out_odd))
    wpg, wpp = w_ple_gate.astype(BF16), w_ple_proj.astype(BF16)
    pool_w16 = pool_w.astype(BF16)
    n_even = sgu_w.shape[0]
    sgu_b4 = sgu_b.reshape(n_even, SGU_GROUPS, SGU_CHUNK, 1)
    sgu_gain4 = sgu_gain.reshape(n_even, SGU_GROUPS, 1, -1)
    pool_scale4 = pool_scale.reshape(pool_scale.shape[0], len(POOL_WINDOWS), 1, -1)
    p3 = p.reshape(depth, t, p.shape[-1])

    xf = x.reshape(t, d)
    for i in range(depth):
        j = i // 2
        xf = _ffn(xf, gains, wg, wu, wd, i, 0, 0, 1)
        if i % 2 == 0:
            z = _norm_matmul(xf, gains, w_in_e, i, 2, j)
            ya, yb = _even_core(z, batch, seq, sgu_w, sgu_b4, sgu_gain4, j)
            xf = _out_proj(ya, yb, xf, gains, w_out_e, i, 3, j)
        else:
            z = _norm_matmul(xf, gains, w_in_o, i, 2, j)
            ya = _pool(z, batch, seq, pool_w16, pool_scale4, j)
            yb = _moba(z, batch, seq)
            xf = _out_proj(ya, yb, xf, gains, w_out_o, i, 3, j)
        xf = _ffn(xf, gains, wg, wu, wd, i, 1, 4, 5)
        xf = _ple(xf, p3, gains, wpg, wpp, i)
    return xf.reshape(batch, seq, d)
```
